```python
import jax, jax.numpy as jnp
from jax import lax
import numpy as np

D_MODEL = 2048
BATCH = 8
SEQ = 2048
DEPTH = 2

N_EVEN = (DEPTH + 1) // 2
N_ODD = DEPTH // 2
MLA_HEADS = 8
Q_LORA = 512
KV_LORA = 512
QK_NOPE = 128
QK_ROPE = 64
V_HEAD = 128
ROPE_BASE = 10000.0
Q_BLOCK = 128
SGU_GROUPS = 8
SGU_CH = 128
CHUNK = 128
CONV_DIM = D_MODEL
CONV_WIDTH = 3
D_FF = 4 * D_MODEL
EPS = 1e-6

MLA_OUT = MLA_HEADS * V_HEAD
SGU_OUT = SGU_GROUPS * SGU_CH
MIX_WIDTH = MLA_OUT + SGU_OUT
EVEN_IN = Q_LORA + KV_LORA + QK_ROPE + 2 * SGU_OUT

kernel_name = "hybrid_mla_sgu_shortconv_block"


def rms_norm(x, g):
    xf = x.astype(jnp.float32)
    y = xf * lax.rsqrt(jnp.mean(xf * xf, axis=-1, keepdims=True) + EPS)
    return (y * g.astype(jnp.float32)).astype(x.dtype)


def group_layer_norm(v, g):
    vf = v.astype(jnp.float32)
    mu = jnp.mean(vf, axis=-1, keepdims=True)
    var = jnp.mean(jnp.square(vf - mu), axis=-1, keepdims=True)
    return ((vf - mu) * lax.rsqrt(var + EPS) * g.astype(jnp.float32)).astype(v.dtype)


def rope_tables(positions):
    inv_freq = ROPE_BASE ** (-jnp.arange(0, QK_ROPE, 2, dtype=jnp.float32) / QK_ROPE)
    ang = positions.astype(jnp.float32)[..., None] * inv_freq
    return jnp.cos(ang), jnp.sin(ang)


def apply_rope(x, cos, sin):
    xf = x.astype(jnp.float32)
    x1, x2 = xf[..., : QK_ROPE // 2], xf[..., QK_ROPE // 2:]
    return jnp.concatenate([x1 * cos - x2 * sin, x2 * cos + x1 * sin], axis=-1).astype(x.dtype)


def mla_mixer(c_q, c_kv, k_rope_raw, cos, sin, q_norm, w_uq, kv_norm, w_ukv):
    B, S, _ = c_q.shape
    q = (rms_norm(c_q, q_norm) @ w_uq).reshape(B, S, MLA_HEADS, QK_NOPE + QK_ROPE)
    q_nope = q[..., :QK_NOPE]
    q_rope = apply_rope(q[..., QK_NOPE:], cos[:, :, None], sin[:, :, None])
    kv = (rms_norm(c_kv, kv_norm) @ w_ukv).reshape(B, S, MLA_HEADS, QK_NOPE + V_HEAD)
    k_nope, v = kv[..., :QK_NOPE], kv[..., QK_NOPE:]
    k_rope = apply_rope(k_rope_raw, cos, sin)
    scale = (QK_NOPE + QK_ROPE) ** -0.5
    outs = []
    for i in range(S // Q_BLOCK):
        q0, kend = i * Q_BLOCK, (i + 1) * Q_BLOCK
        s = (jnp.einsum('bqhd,bkhd->bhqk', q_nope[:, q0:kend], k_nope[:, :kend])
             + jnp.einsum('bqhr,bkr->bhqk', q_rope[:, q0:kend], k_rope[:, :kend]))
        s = s.astype(jnp.float32) * scale
        q_idx = q0 + jnp.arange(Q_BLOCK)
        mask = jnp.arange(kend)[None, :] <= q_idx[:, None]
        p = jax.nn.softmax(jnp.where(mask, s, -jnp.inf), axis=-1).astype(v.dtype)
        outs.append(jnp.einsum('bhqk,bkhd->bqhd', p, v[:, :kend]))
    o = jnp.concatenate(outs, axis=1)
    return o.reshape(B, S, MLA_OUT)


def sgu_mixer(uv, v_norm, w_s, b_s):
    B, S, _ = uv.shape
    uv = jax.nn.gelu(uv)
    u, v = uv[..., :SGU_OUT], uv[..., SGU_OUT:]
    v = group_layer_norm(v.reshape(B, S, SGU_GROUPS, SGU_CH), v_norm)
    v = v.reshape(B, S // CHUNK, CHUNK, SGU_GROUPS, SGU_CH)
    w = jnp.tril(w_s)
    y = jnp.einsum('gts,bcsgd->bctgd', w, v) + b_s.T[None, None, :, :, None]
    return u * y.reshape(B, S, SGU_OUT)


def short_conv_mixer(h, w_in, conv_w, w_out):
    proj = h @ w_in
    b_gate = proj[..., :CONV_DIM]
    c_gate = proj[..., CONV_DIM:2 * CONV_DIM]
    xin = proj[..., 2 * CONV_DIM:]
    z = c_gate * xin
    z = lax.conv_general_dilated(
        z, conv_w[:, None, :].astype(z.dtype), window_strides=(1,),
        padding=[(CONV_WIDTH - 1, 0)], dimension_numbers=('NWC', 'WIO', 'NWC'),
        feature_group_count=CONV_DIM)
    return (b_gate * z) @ w_out


def sqrelu_mlp(h, w1, w2):
    a = jax.nn.relu(h @ w1)
    return (a * a) @ w2


def setup_inputs(seed: int = 0) -> dict:
    key = jax.random.key(seed)
    ks = jax.random.split(key, 24)

    def nrm(k, shape, scale):
        return jax.random.normal(k, shape, jnp.float32) * scale

    def gain(k, shape):
        return 1.0 + 0.02 * jax.random.normal(k, shape, jnp.float32)

    x = jax.random.normal(ks[0], (BATCH, SEQ, D_MODEL), jnp.float32)
    offset = jax.random.randint(ks[1], (BATCH, 1), 0, 4096, dtype=jnp.int32)
    positions = offset + jnp.arange(SEQ, dtype=jnp.int32)[None, :]
    return {
        "x": x,
        "positions": positions,
        "e_norm_mix": gain(ks[2], (N_EVEN, D_MODEL)),
        "e_w_in": nrm(ks[3], (N_EVEN, D_MODEL, EVEN_IN), D_MODEL ** -0.5),
        "e_q_norm": gain(ks[4], (N_EVEN, Q_LORA)),
        "e_w_uq": nrm(ks[5], (N_EVEN, Q_LORA, MLA_HEADS * (QK_NOPE + QK_ROPE)), Q_LORA ** -0.5),
        "e_kv_norm": gain(ks[6], (N_EVEN, KV_LORA)),
        "e_w_ukv": nrm(ks[7], (N_EVEN, KV_LORA, MLA_HEADS * (QK_NOPE + V_HEAD)), KV_LORA ** -0.5),
        "e_v_norm": gain(ks[8], (N_EVEN, SGU_GROUPS, SGU_CH)),
        "e_sgu_w": nrm(ks[9], (N_EVEN, SGU_GROUPS, CHUNK, CHUNK), CHUNK ** -0.5),
        "e_sgu_b": 1.0 + nrm(ks[10], (N_EVEN, SGU_GROUPS, CHUNK), 0.1),
        "e_mla_out_norm": gain(ks[11], (N_EVEN, MLA_OUT)),
        "e_sgu_out_norm": gain(ks[12], (N_EVEN, SGU_OUT)),
        "e_w_out": nrm(ks[13], (N_EVEN, MIX_WIDTH, D_MODEL), MIX_WIDTH ** -0.5),
        "o_norm_mix": gain(ks[14], (N_ODD, D_MODEL)),
        "o_w_in": nrm(ks[15], (N_ODD, D_MODEL, 3 * CONV_DIM), D_MODEL ** -0.5),
        "o_conv_w": nrm(ks[16], (N_ODD, CONV_WIDTH, CONV_DIM), CONV_WIDTH ** -0.5),
        "o_w_out": nrm(ks[17], (N_ODD, CONV_DIM, D_MODEL), CONV_DIM ** -0.5),
        "mlp_norm": gain(ks[18], (DEPTH, D_MODEL)),
        "mlp_w1": nrm(ks[19], (DEPTH, D_MODEL, D_FF), D_MODEL ** -0.5),
        "mlp_w2": nrm(ks[20], (DEPTH, D_FF, D_MODEL), 0.5 * D_FF ** -0.5),
        "final_norm": gain(ks[21], (D_MODEL,)),
    }


def reference(x, positions, e_norm_mix, e_w_in, e_q_norm, e_w_uq, e_kv_norm, e_w_ukv,
              e_v_norm, e_sgu_w, e_sgu_b, e_mla_out_norm, e_sgu_out_norm, e_w_out,
              o_norm_mix, o_w_in, o_conv_w, o_w_out, mlp_norm, mlp_w1, mlp_w2, final_norm):
    cos, sin = rope_tables(positions)
    c1 = Q_LORA
    c2 = c1 + KV_LORA
    c3 = c2 + QK_ROPE
    for layer in range(DEPTH):
        i = layer // 2
        if layer % 2 == 0:
            h = rms_norm(x, e_norm_mix[i])
            proj = h @ e_w_in[i]
            a = mla_mixer(proj[..., :c1], proj[..., c1:c2], proj[..., c2:c3], cos, sin,
                          e_q_norm[i], e_w_uq[i], e_kv_norm[i], e_w_ukv[i])
            s = sgu_mixer(proj[..., c3:], e_v_norm[i], e_sgu_w[i], e_sgu_b[i])
            mixed = jnp.concatenate([rms_norm(a, e_mla_out_norm[i]),
                                     rms_norm(s, e_sgu_out_norm[i])], axis=-1)
            x = x + mixed @ e_w_out[i]
        else:
            x = x + short_conv_mixer(rms_norm(x, o_norm_mix[i]), o_w_in[i], o_conv_w[i], o_w_out[i])
        x = x + sqrelu_mlp(rms_norm(x, mlp_norm[layer]), mlp_w1[layer], mlp_w2[layer])
    return rms_norm(x, final_norm)
```

```python
import functools

import jax
import jax.numpy as jnp
from jax import lax
from jax.experimental import pallas as pl
from jax.experimental.pallas import tpu as pltpu

D_MODEL = 2048
MLA_HEADS = 8
Q_LORA = 512
KV_LORA = 512
QK_NOPE = 128
QK_ROPE = 64
V_HEAD = 128
ROPE_BASE = 10000.0
SGU_GROUPS = 8
SGU_CH = 128
CHUNK = 128
CONV_DIM = D_MODEL
CONV_WIDTH = 3
D_FF = 4 * D_MODEL
EPS = 1e-6
MLA_OUT = MLA_HEADS * V_HEAD
SGU_OUT = SGU_GROUPS * SGU_CH
HALF_ROPE = QK_ROPE // 2

LANES = 128
VMEM_LIMIT_BYTES = 60 * 1024 * 1024

FRONT_ROWS = 256
ATTN_BLOCK = 256
MLP_ROWS = 512
MLP_FF = 512
CONV_ROWS = 512
CONV_COLS = 512

BF16 = jnp.bfloat16
F32 = jnp.float32


def _dot(a, b):
    return jnp.dot(a, b, preferred_element_type=F32)


def _dot_nt(a, b):
    return lax.dot_general(a, b, (((1,), (1,)), ((), ())), preferred_element_type=F32)


def _rms(x, g):
    return x * lax.rsqrt(jnp.mean(x * x, axis=-1, keepdims=True) + EPS) * g


def _const_spec(shape):
    return pl.BlockSpec(shape, lambda *_: (0,) * len(shape), pipeline_mode=pl.Buffered(1))


def _even_front_kernel(x_ref, pos_ref, invf_ref, sign_ref, g_ref, win_ref, qg_ref, wuq_ref,
                       kvg_ref, wukv_ref, vng_ref, sguw_ref, sgub_ref, sgn_ref,
                       qn_ref, qr_ref, kn_ref, kr_ref, v_ref, s_ref,
                       vn_buf, s_buf):
    tm = x_ref.shape[0]
    scale = (QK_NOPE + QK_ROPE) ** -0.5
    h = _rms(x_ref[...], g_ref[...]).astype(BF16)

    ang = pos_ref[...].astype(F32) * invf_ref[...]
    cos = jnp.cos(ang)
    sin = jnp.sin(ang) * sign_ref[...]

    cq = _rms(_dot(h, win_ref[:, 0:Q_LORA]), qg_ref[...]).astype(BF16)
    q = _dot(cq, wuq_ref[...])
    n_rope = MLA_HEADS * QK_ROPE
    qn_ref[...] = (q[:, :MLA_OUT] * scale).astype(BF16)
    cos_q = jnp.concatenate([cos] * (n_rope // LANES), axis=1)
    sin_q = jnp.concatenate([sin] * (n_rope // LANES), axis=1)
    qa = q[:, MLA_OUT:MLA_OUT + n_rope]
    qb = q[:, MLA_OUT + n_rope:MLA_OUT + 2 * n_rope]
    qr_ref[...] = ((qa * cos_q + qb * sin_q) * scale).astype(BF16)

    c0 = Q_LORA
    ckv = _rms(_dot(h, win_ref[:, c0:c0 + KV_LORA]), kvg_ref[...]).astype(BF16)
    kv = _dot(ckv, wukv_ref[...])
    kn_ref[...] = kv[:, :MLA_OUT].astype(BF16)
    v_ref[...] = kv[:, MLA_OUT:].astype(BF16)

    c0 = Q_LORA + KV_LORA + 2 * SGU_OUT
    ka = _dot(h, win_ref[:, c0:c0 + LANES])
    kb = _dot(h, win_ref[:, c0 + LANES:c0 + 2 * LANES])
    kr_ref[...] = (ka * cos + kb * sin).astype(BF16)

    c0 = Q_LORA + KV_LORA
    u = jax.nn.gelu(_dot(h, win_ref[:, c0:c0 + SGU_OUT]))
    vv = jax.nn.gelu(_dot(h, win_ref[:, c0 + SGU_OUT:c0 + 2 * SGU_OUT]))
    for g in range(SGU_GROUPS):
        lo = g * SGU_CH
        vg = vv[:, lo:lo + SGU_CH]
        mu = jnp.mean(vg, axis=-1, keepdims=True)
        d = vg - mu
        var = jnp.mean(d * d, axis=-1, keepdims=True)
        vn_buf[:, lo:lo + SGU_CH] = (d * lax.rsqrt(var + EPS) * vng_ref[:, lo:lo + SGU_CH]).astype(BF16)
    row = lax.broadcasted_iota(jnp.int32, (CHUNK, CHUNK), 0)
    col = lax.broadcasted_iota(jnp.int32, (CHUNK, CHUNK), 1)
    causal = col <= row
    for g in range(SGU_GROUPS):
        lo = g * SGU_CH
        w = jnp.where(causal, sguw_ref[g], 0.0).astype(BF16)
        for c in range(tm // CHUNK):
            r0 = c * CHUNK
            y = _dot(w, vn_buf[r0:r0 + CHUNK, lo:lo + SGU_CH])
            s_buf[r0:r0 + CHUNK, lo:lo + SGU_CH] = y + sgub_ref[:, lo:lo + SGU_CH]
    s_ref[...] = _rms(u * s_buf[...], sgn_ref[...]).astype(BF16)


def _even_front(x, pos, invf, sign, g, win, qg, wuq, kvg, wukv, vng, sguw, sgub, sgn):
    T = x.shape[0]
    tm = FRONT_ROWS
    row = lambda w: pl.BlockSpec((tm, w), lambda i: (i, 0))
    out_widths = (MLA_OUT, MLA_HEADS * QK_ROPE, MLA_OUT, LANES, MLA_OUT, SGU_OUT)
    return pl.pallas_call(
        _even_front_kernel,
        grid=(T // tm,),
        in_specs=[row(D_MODEL), row(1), _const_spec(invf.shape), _const_spec(sign.shape),
                  _const_spec(g.shape), _const_spec(win.shape), _const_spec(qg.shape),
                  _const_spec(wuq.shape), _const_spec(kvg.shape), _const_spec(wukv.shape),
                  _const_spec(vng.shape), _const_spec(sguw.shape), _const_spec(sgub.shape),
                  _const_spec(sgn.shape)],
        out_specs=[row(w) for w in out_widths],
        out_shape=[jax.ShapeDtypeStruct((T, w), BF16) for w in out_widths],
        scratch_shapes=[pltpu.VMEM((tm, SGU_OUT), BF16), pltpu.VMEM((tm, SGU_OUT), F32)],
        compiler_params=pltpu.CompilerParams(dimension_semantics=("parallel",),
                                             vmem_limit_bytes=VMEM_LIMIT_BYTES),
        name="even_front",
    )(x, pos, invf, sign, g, win, qg, wuq, kvg, wukv, vng, sguw, sgub, sgn)


def _attn_kernel(qn_ref, qr_ref, kn_ref, kr_ref, v_ref, g_ref, o_ref, a_buf):
    tq = qn_ref.shape[0]
    tk = tq
    qi = pl.program_id(1)
    row = lax.broadcasted_iota(jnp.int32, (tq, tk), 0)
    col = lax.broadcasted_iota(jnp.int32, (tq, tk), 1)
    causal = col <= row

    for h in range(MLA_HEADS):
        nlo = h * QK_NOPE
        qn = qn_ref[:, nlo:nlo + QK_NOPE]
        qr = qr_ref[:, h * QK_ROPE:(h + 1) * QK_ROPE]

        def scores(j):
            off = pl.multiple_of(j * tk, tk)
            kn = kn_ref[pl.ds(off, tk), nlo:nlo + QK_NOPE]
            kr = kr_ref[pl.ds(off, tk), 0:QK_ROPE]
            return _dot_nt(qn, kn) + _dot_nt(qr, kr), off

        def update(carry, s, off):
            m, l, acc = carry
            m_new = jnp.maximum(m, jnp.max(s, axis=-1, keepdims=True))
            alpha = jnp.exp(m - m_new)
            p = jnp.exp(s - m_new)
            l = alpha * l + jnp.sum(p, axis=-1, keepdims=True)
            vv = v_ref[pl.ds(off, tk), h * V_HEAD:(h + 1) * V_HEAD]
            acc = alpha * acc + _dot(p.astype(BF16), vv)
            return m_new, l, acc

        def body(j, carry):
            s, off = scores(j)
            return update(carry, s, off)

        init = (jnp.full((tq, 1), -jnp.inf, F32), jnp.zeros((tq, 1), F32),
                jnp.zeros((tq, V_HEAD), F32))
        carry = lax.fori_loop(0, qi, body, init)
        s, off = scores(qi)
        _, l, acc = update(carry, jnp.where(causal, s, -jnp.inf), off)
        a_buf[:, h * V_HEAD:(h + 1) * V_HEAD] = acc / l

    o_ref[...] = _rms(a_buf[...], g_ref[...]).astype(BF16)


def _attention(qn, qr, kn, kr, v, g, batch, seq):
    T = qn.shape[0]
    tq = ATTN_BLOCK
    nq = seq // tq
    qspec = lambda w: pl.BlockSpec((tq, w), lambda b, i: (b * nq + i, 0))
    kspec = lambda w: pl.BlockSpec((seq, w), lambda b, i: (b, 0))
    return pl.pallas_call(
        _attn_kernel,
        grid=(batch, nq),
        in_specs=[qspec(MLA_OUT), qspec(MLA_HEADS * QK_ROPE), kspec(MLA_OUT), kspec(LANES),
                  kspec(MLA_OUT), _const_spec(g.shape)],
        out_specs=qspec(MLA_OUT),
        out_shape=jax.ShapeDtypeStruct((T, MLA_OUT), BF16),
        scratch_shapes=[pltpu.VMEM((tq, MLA_OUT), F32)],
        compiler_params=pltpu.CompilerParams(dimension_semantics=("parallel", "parallel"),
                                             vmem_limit_bytes=VMEM_LIMIT_BYTES),
        name="mla_attention",
    )(qn, qr, kn, kr, v, g)


def _out_mlp_kernel(x_ref, ma_ref, mb_ref, wo_ref, g_ref, w1_ref, w2_ref, fg_ref, o_ref, h_buf,
                    *, final_norm):
    k = pl.program_id(1)
    half = ma_ref.shape[1]

    @pl.when(k == 0)
    def _():
        x1 = (x_ref[...] + _dot(ma_ref[...], wo_ref[0:half, :])
              + _dot(mb_ref[...], wo_ref[half:2 * half, :]))
        o_ref[...] = x1
        h_buf[...] = _rms(x1, g_ref[...]).astype(BF16)

    a = jnp.maximum(_dot(h_buf[...], w1_ref[...]), 0.0)
    o_ref[...] += _dot((a * a).astype(BF16), w2_ref[...])

    if final_norm:
        @pl.when(k == pl.num_programs(1) - 1)
        def _():
            o_ref[...] = _rms(o_ref[...], fg_ref[...])


def _out_mlp(x, ma, mb, wo, g, w1, w2, fg, *, final_norm):
    T = x.shape[0]
    tm, tf = MLP_ROWS, MLP_FF
    half = wo.shape[0] // 2
    return pl.pallas_call(
        functools.partial(_out_mlp_kernel, final_norm=final_norm),
        grid=(T // tm, D_FF // tf),
        in_specs=[pl.BlockSpec((tm, D_MODEL), lambda i, k: (i, 0)),
                  pl.BlockSpec((tm, half), lambda i, k: (i, 0)),
                  pl.BlockSpec((tm, half), lambda i, k: (i, mb.shape[1] // half - 1)),
                  _const_spec(wo.shape), _const_spec(g.shape),
                  pl.BlockSpec((D_MODEL, tf), lambda i, k: (0, k)),
                  pl.BlockSpec((tf, D_MODEL), lambda i, k: (k, 0)),
                  _const_spec(fg.shape)],
        out_specs=pl.BlockSpec((tm, D_MODEL), lambda i, k: (i, 0)),
        out_shape=jax.ShapeDtypeStruct((T, D_MODEL), F32),
        scratch_shapes=[pltpu.VMEM((tm, D_MODEL), BF16)],
        compiler_params=pltpu.CompilerParams(dimension_semantics=("parallel", "arbitrary"),
                                             vmem_limit_bytes=VMEM_LIMIT_BYTES),
        name="out_proj_mlp",
    )(x, ma, mb, wo, g, w1, w2, fg)


def _conv_front_kernel(x_ref, g_ref, wb_ref, wc_ref, wx_ref, cw_ref, o_ref, hist, *, tiles_per_seq):
    tm, tn = o_ref.shape
    i = pl.program_id(1)
    h = _rms(x_ref[...], g_ref[...]).astype(BF16)
    z = _dot(h, wc_ref[...]) * _dot(h, wx_ref[...])

    @pl.when(i % tiles_per_seq == 0)
    def _():
        hist[...] = jnp.zeros_like(hist)

    row = lax.broadcasted_iota(jnp.int32, (tm, tn), 0)
    p1 = hist[7:8, :]
    p2 = hist[6:7, :]
    z1 = jnp.where(row == 0, p1, pltpu.roll(z, 1, 0))
    z2 = jnp.where(row == 0, p2, jnp.where(row == 1, p1, pltpu.roll(z, 2, 0)))
    conv = cw_ref[0:1, :] * z2 + cw_ref[1:2, :] * z1 + cw_ref[2:3, :] * z
    o_ref[...] = (_dot(h, wb_ref[...]) * conv).astype(BF16)
    hist[...] = z[tm - 8:tm, :]


def _conv_front(x, g, w_in, conv_w, seq):
    T = x.shape[0]
    tm, tn = CONV_ROWS, CONV_COLS
    nb = CONV_DIM // tn
    wspec = lambda part: pl.BlockSpec((D_MODEL, tn), lambda j, i: (0, part * nb + j))
    return pl.pallas_call(
        functools.partial(_conv_front_kernel, tiles_per_seq=seq // tm),
        grid=(nb, T // tm),
        in_specs=[pl.BlockSpec((tm, D_MODEL), lambda j, i: (i, 0)), _const_spec(g.shape),
                  wspec(0), wspec(1), wspec(2),
                  pl.BlockSpec((CONV_WIDTH, tn), lambda j, i: (0, j))],
        out_specs=pl.BlockSpec((tm, tn), lambda j, i: (i, j)),
        out_shape=jax.ShapeDtypeStruct((T, CONV_DIM), BF16),
        scratch_shapes=[pltpu.VMEM((8, tn), F32)],
        compiler_params=pltpu.CompilerParams(dimension_semantics=("arbitrary", "arbitrary"),
                                             vmem_limit_bytes=VMEM_LIMIT_BYTES),
        name="conv_front",
    )(x, g, w_in, w_in, w_in, conv_w)


def _even_weights(w_in, w_uq, w_ukv):
    c1 = Q_LORA
    c2 = c1 + KV_LORA
    c3 = c2 + QK_ROPE
    k1 = w_in[:, c2:c2 + HALF_ROPE]
    k2 = w_in[:, c2 + HALF_ROPE:c3]
    win = jnp.concatenate([w_in[:, :c2], w_in[:, c3:], k1, k2, k1, k2, k2, k1, k2, k1], axis=1)
    wq = w_uq.reshape(Q_LORA, MLA_HEADS, QK_NOPE + QK_ROPE)
    r1 = wq[:, :, QK_NOPE:QK_NOPE + HALF_ROPE]
    r2 = wq[:, :, QK_NOPE + HALF_ROPE:]
    wuq = jnp.concatenate([
        wq[:, :, :QK_NOPE].reshape(Q_LORA, -1),
        jnp.concatenate([r1, r2], axis=-1).reshape(Q_LORA, -1),
        jnp.concatenate([r2, r1], axis=-1).reshape(Q_LORA, -1)], axis=1)
    wkv = w_ukv.reshape(KV_LORA, MLA_HEADS, QK_NOPE + V_HEAD)
    wukv = jnp.concatenate([wkv[:, :, :QK_NOPE].reshape(KV_LORA, -1),
                            wkv[:, :, QK_NOPE:].reshape(KV_LORA, -1)], axis=1)
    return win.astype(BF16), wuq.astype(BF16), wukv.astype(BF16)


def _rope_rows():
    lane = jnp.arange(LANES)
    inv_freq = ROPE_BASE ** (-jnp.arange(0, QK_ROPE, 2, dtype=F32) / QK_ROPE)
    invf = inv_freq[lane % HALF_ROPE][None, :]
    sign = jnp.where((lane % QK_ROPE) < HALF_ROPE, -1.0, 1.0).astype(F32)[None, :]
    return invf, sign


def kernel(x, positions, e_norm_mix, e_w_in, e_q_norm, e_w_uq, e_kv_norm, e_w_ukv, e_v_norm,
           e_sgu_w, e_sgu_b, e_mla_out_norm, e_sgu_out_norm, e_w_out, o_norm_mix, o_w_in,
           o_conv_w, o_w_out, mlp_norm, mlp_w1, mlp_w2, final_norm):
    batch, seq, d = x.shape
    T = batch * seq
    depth = mlp_norm.shape[0]
    xf = x.reshape(T, d)
    pos = positions.reshape(T, 1)
    invf, sign = _rope_rows()
    fg = final_norm[None, :]
    row2d = lambda a: a.reshape(1, -1)

    for layer in range(depth):
        i = layer // 2
        if layer % 2 == 0:
            win, wuq, wukv = _even_weights(e_w_in[i], e_w_uq[i], e_w_ukv[i])
            sgub = jnp.repeat(e_sgu_b[i].T, SGU_CH, axis=1)
            qn, qr, kn, kr, v, sn = _even_front(
                xf, pos, invf, sign, row2d(e_norm_mix[i]), win, row2d(e_q_norm[i]), wuq,
                row2d(e_kv_norm[i]), wukv, row2d(e_v_norm[i]), e_sgu_w[i], sgub,
                row2d(e_sgu_out_norm[i]))
            an = _attention(qn, qr, kn, kr, v, row2d(e_mla_out_norm[i]), batch, seq)
            ma, mb = an, sn
            wo = e_w_out[i].astype(BF16)
        else:
            gated = _conv_front(xf, row2d(o_norm_mix[i]), o_w_in[i].astype(BF16), o_conv_w[i], seq)
            ma, mb = gated, gated
            wo = o_w_out[i].astype(BF16)
        xf = _out_mlp(xf, ma, mb, wo, row2d(mlp_norm[layer]), mlp_w1[layer].astype(BF16),
                      mlp_w2[layer].astype(BF16), fg, final_norm=(layer == depth - 1))
    return xf.reshape(batch, seq, d)
```

```python
import functools

import jax
import jax.numpy as jnp
from jax import lax
from jax.experimental import pallas as pl
from jax.experimental.pallas import tpu as pltpu

D_MODEL = 2048
MLA_HEADS = 8
Q_LORA = 512
KV_LORA = 512
QK_NOPE = 128
QK_ROPE = 64
V_HEAD = 128
ROPE_BASE = 10000.0
SGU_GROUPS = 8
SGU_CH = 128
CHUNK = 128
CONV_DIM = D_MODEL
CONV_WIDTH = 3
D_FF = 4 * D_MODEL
EPS = 1e-6
MLA_OUT = MLA_HEADS * V_HEAD
SGU_OUT = SGU_GROUPS * SGU_CH
HALF_ROPE = QK_ROPE // 2
HEAD_SLOT = 2 * QK_NOPE

LANES = 128
VMEM_LIMIT_BYTES = 60 * 1024 * 1024

FRONT_ROWS = 256
ATTN_BLOCK = 256
MLP_ROWS = 512
MLP_FF = 1024
CONV_ROWS = 512
CONV_COLS = 512

BF16 = jnp.bfloat16
F32 = jnp.float32


def _dot(a, b):
    return jnp.dot(a, b, preferred_element_type=F32)


def _dot_nt(a, b):
    return lax.dot_general(a, b, (((1,), (1,)), ((), ())), preferred_element_type=F32)


def _rms(x, g):
    return x * lax.rsqrt(jnp.mean(x * x, axis=-1, keepdims=True) + EPS) * g


def _const_spec(shape):
    return pl.BlockSpec(shape, lambda *_: (0,) * len(shape), pipeline_mode=pl.Buffered(1))


def _even_front_kernel(x_ref, pos_ref, invf_ref, sign_ref, g_ref, win_ref, qg_ref, wuq_ref,
                       kvg_ref, wukv_ref, vng_ref, sguw_ref, sgub_ref, sgn_ref,
                       q_ref, k_ref, v_ref, s_ref,
                       vn_buf, s_buf):
    tm = x_ref.shape[0]
    scale = (QK_NOPE + QK_ROPE) ** -0.5
    h = _rms(x_ref[...], g_ref[...]).astype(BF16)

    ang = pos_ref[...].astype(F32) * invf_ref[...]
    cos = jnp.cos(ang)
    sin = jnp.sin(ang) * sign_ref[...]

    cq = _rms(_dot(h, win_ref[:, 0:Q_LORA]), qg_ref[...]).astype(BF16)
    q = _dot(cq, wuq_ref[...])
    cos_q = jnp.concatenate([cos] * MLA_HEADS, axis=1)
    sin_q = jnp.concatenate([sin] * MLA_HEADS, axis=1)
    qn = (q[:, :MLA_OUT] * scale).astype(BF16)
    qr = ((q[:, MLA_OUT:2 * MLA_OUT] * cos_q + q[:, 2 * MLA_OUT:] * sin_q) * scale).astype(BF16)

    c0 = Q_LORA
    ckv = _rms(_dot(h, win_ref[:, c0:c0 + KV_LORA]), kvg_ref[...]).astype(BF16)
    kv = _dot(ckv, wukv_ref[...])
    kn = kv[:, :MLA_OUT].astype(BF16)
    v_ref[...] = kv[:, MLA_OUT:].astype(BF16)

    c0 = Q_LORA + KV_LORA + 2 * SGU_OUT
    ka = _dot(h, win_ref[:, c0:c0 + LANES])
    kb = _dot(h, win_ref[:, c0 + LANES:c0 + 2 * LANES])
    kr = (ka * cos + kb * sin).astype(BF16)
    for hd in range(MLA_HEADS):
        lo = hd * HEAD_SLOT
        src = slice(hd * QK_NOPE, (hd + 1) * QK_NOPE)
        q_ref[:, lo:lo + QK_NOPE] = qn[:, src]
        q_ref[:, lo + QK_NOPE:lo + HEAD_SLOT] = qr[:, src]
        k_ref[:, lo:lo + QK_NOPE] = kn[:, src]
        k_ref[:, lo + QK_NOPE:lo + HEAD_SLOT] = kr

    c0 = Q_LORA + KV_LORA
    u = jax.nn.gelu(_dot(h, win_ref[:, c0:c0 + SGU_OUT]))
    vv = jax.nn.gelu(_dot(h, win_ref[:, c0 + SGU_OUT:c0 + 2 * SGU_OUT]))
    for g in range(SGU_GROUPS):
        lo = g * SGU_CH
        vg = vv[:, lo:lo + SGU_CH]
        mu = jnp.mean(vg, axis=-1, keepdims=True)
        d = vg - mu
        var = jnp.mean(d * d, axis=-1, keepdims=True)
        vn_buf[:, lo:lo + SGU_CH] = (d * lax.rsqrt(var + EPS) * vng_ref[:, lo:lo + SGU_CH]).astype(BF16)
    row = lax.broadcasted_iota(jnp.int32, (CHUNK, CHUNK), 0)
    col = lax.broadcasted_iota(jnp.int32, (CHUNK, CHUNK), 1)
    causal = col <= row
    for g in range(SGU_GROUPS):
        lo = g * SGU_CH
        w = jnp.where(causal, sguw_ref[g], 0.0).astype(BF16)
        for c in range(tm // CHUNK):
            r0 = c * CHUNK
            y = _dot(w, vn_buf[r0:r0 + CHUNK, lo:lo + SGU_CH])
            s_buf[r0:r0 + CHUNK, lo:lo + SGU_CH] = y + sgub_ref[:, lo:lo + SGU_CH]
    s_ref[...] = _rms(u * s_buf[...], sgn_ref[...]).astype(BF16)


def _even_front(x, pos, invf, sign, g, win, qg, wuq, kvg, wukv, vng, sguw, sgub, sgn):
    T = x.shape[0]
    tm = FRONT_ROWS
    row = lambda w: pl.BlockSpec((tm, w), lambda i: (i, 0))
    out_widths = (MLA_HEADS * HEAD_SLOT, MLA_HEADS * HEAD_SLOT, MLA_OUT, SGU_OUT)
    return pl.pallas_call(
        _even_front_kernel,
        grid=(T // tm,),
        in_specs=[row(D_MODEL), row(1), _const_spec(invf.shape), _const_spec(sign.shape),
                  _const_spec(g.shape), _const_spec(win.shape), _const_spec(qg.shape),
                  _const_spec(wuq.shape), _const_spec(kvg.shape), _const_spec(wukv.shape),
                  _const_spec(vng.shape), _const_spec(sguw.shape), _const_spec(sgub.shape),
                  _const_spec(sgn.shape)],
        out_specs=[row(w) for w in out_widths],
        out_shape=[jax.ShapeDtypeStruct((T, w), BF16) for w in out_widths],
        scratch_shapes=[pltpu.VMEM((tm, SGU_OUT), BF16), pltpu.VMEM((tm, SGU_OUT), F32)],
        compiler_params=pltpu.CompilerParams(dimension_semantics=("parallel",),
                                             vmem_limit_bytes=VMEM_LIMIT_BYTES),
        name="even_front",
    )(x, pos, invf, sign, g, win, qg, wuq, kvg, wukv, vng, sguw, sgub, sgn)


def _attn_kernel(q_ref, k_ref, v_ref, o_ref):
    seq = q_ref.shape[0]
    tq = ATTN_BLOCK
    row = lax.broadcasted_iota(jnp.int32, (tq, tq), 0)
    col = lax.broadcasted_iota(jnp.int32, (tq, tq), 1)
    causal = col <= row
    for qi in range(seq // tq):
        r0 = qi * tq
        r1 = r0 + tq
        q = q_ref[r0:r1, :]
        sd = jnp.where(causal, _dot_nt(q, k_ref[r0:r1, :]), -jnp.inf)
        m = jnp.max(sd, axis=-1, keepdims=True)
        if qi:
            sp = _dot_nt(q, k_ref[0:r0, :])
            m = jnp.maximum(m, jnp.max(sp, axis=-1, keepdims=True))
        pd = jnp.exp(sd - m)
        l = jnp.sum(pd, axis=-1, keepdims=True)
        acc = _dot(pd.astype(BF16), v_ref[r0:r1, :])
        if qi:
            pp = jnp.exp(sp - m)
            l = l + jnp.sum(pp, axis=-1, keepdims=True)
            acc = acc + _dot(pp.astype(BF16), v_ref[0:r0, :])
        o_ref[r0:r1, :] = (acc / l).astype(BF16)


def _attention(q, k, v, batch, seq):
    T = q.shape[0]
    spec = lambda w: pl.BlockSpec((seq, w), lambda b, h: (b, h))
    return pl.pallas_call(
        _attn_kernel,
        grid=(batch, MLA_HEADS),
        in_specs=[spec(HEAD_SLOT), spec(HEAD_SLOT), spec(V_HEAD)],
        out_specs=spec(V_HEAD),
        out_shape=jax.ShapeDtypeStruct((T, MLA_OUT), BF16),
        compiler_params=pltpu.CompilerParams(dimension_semantics=("parallel", "parallel"),
                                             vmem_limit_bytes=VMEM_LIMIT_BYTES),
        name="mla_attention",
    )(q, k, v)


def _out_mlp_kernel(x_ref, ma_ref, mb_ref, wo_ref, g_ref, w1_ref, w2_ref, fg_ref, *rest,
                    norm_a, final_norm):
    if norm_a:
        ag_ref, o_ref, h_buf = rest
    else:
        o_ref, h_buf = rest
    k = pl.program_id(1)
    half = ma_ref.shape[1]

    @pl.when(k == 0)
    def _():
        ma = ma_ref[...]
        if norm_a:
            ma = _rms(ma.astype(F32), ag_ref[...]).astype(BF16)
        x1 = (x_ref[...] + _dot(ma, wo_ref[0:half, :])
              + _dot(mb_ref[...], wo_ref[half:2 * half, :]))
        o_ref[...] = x1
        h_buf[...] = _rms(x1, g_ref[...]).astype(BF16)

    a = jnp.maximum(_dot(h_buf[...], w1_ref[...]), 0.0)
    o_ref[...] += _dot((a * a).astype(BF16), w2_ref[...])

    if final_norm:
        @pl.when(k == pl.num_programs(1) - 1)
        def _():
            o_ref[...] = _rms(o_ref[...], fg_ref[...])


def _out_mlp(x, ma, mb, wo, wo_layer, g, w1, w2, layer, fg, a_gain, *, final_norm):
    T = x.shape[0]
    tm, tf = MLP_ROWS, MLP_FF
    half = wo.shape[1] // 2
    norm_a = a_gain is not None
    in_specs = [pl.BlockSpec((tm, D_MODEL), lambda i, k: (i, 0)),
                pl.BlockSpec((tm, half), lambda i, k: (i, 0)),
                pl.BlockSpec((tm, half), lambda i, k: (i, mb.shape[1] // half - 1)),
                pl.BlockSpec((None,) + wo.shape[1:], lambda i, k: (wo_layer, 0, 0),
                             pipeline_mode=pl.Buffered(1)),
                _const_spec(g.shape),
                pl.BlockSpec((None, D_MODEL, tf), lambda i, k: (layer, 0, k)),
                pl.BlockSpec((None, tf, D_MODEL), lambda i, k: (layer, k, 0)),
                _const_spec(fg.shape)]
    args = [x, ma, mb, wo, g, w1, w2, fg]
    if norm_a:
        in_specs.append(_const_spec(a_gain.shape))
        args.append(a_gain)
    return pl.pallas_call(
        functools.partial(_out_mlp_kernel, norm_a=norm_a, final_norm=final_norm),
        grid=(T // tm, D_FF // tf),
        in_specs=in_specs,
        out_specs=pl.BlockSpec((tm, D_MODEL), lambda i, k: (i, 0)),
        out_shape=jax.ShapeDtypeStruct((T, D_MODEL), F32),
        scratch_shapes=[pltpu.VMEM((tm, D_MODEL), BF16)],
        compiler_params=pltpu.CompilerParams(dimension_semantics=("parallel", "arbitrary"),
                                             vmem_limit_bytes=VMEM_LIMIT_BYTES),
        name="out_proj_mlp",
    )(*args)


def _conv_front_kernel(x_ref, g_ref, wb_ref, wc_ref, wx_ref, cw_ref, o_ref, hist, *, tiles_per_seq):
    tm, tn = o_ref.shape
    i = pl.program_id(1)
    h = _rms(x_ref[...], g_ref[...]).astype(BF16)
    z = _dot(h, wc_ref[...]) * _dot(h, wx_ref[...])

    @pl.when(i % tiles_per_seq == 0)
    def _():
        hist[...] = jnp.zeros_like(hist)

    row = lax.broadcasted_iota(jnp.int32, (tm, tn), 0)
    p1 = hist[7:8, :]
    p2 = hist[6:7, :]
    z1 = jnp.where(row == 0, p1, pltpu.roll(z, 1, 0))
    z2 = jnp.where(row == 0, p2, jnp.where(row == 1, p1, pltpu.roll(z, 2, 0)))
    conv = cw_ref[0:1, :] * z2 + cw_ref[1:2, :] * z1 + cw_ref[2:3, :] * z
    o_ref[...] = (_dot(h, wb_ref[...]) * conv).astype(BF16)
    hist[...] = z[tm - 8:tm, :]


def _conv_front(x, g, w_in, conv_w, layer, seq):
    T = x.shape[0]
    tm, tn = CONV_ROWS, CONV_COLS
    nb = CONV_DIM // tn
    wspec = lambda part: pl.BlockSpec((None, D_MODEL, tn), lambda j, i: (layer, 0, part * nb + j))
    return pl.pallas_call(
        functools.partial(_conv_front_kernel, tiles_per_seq=seq // tm),
        grid=(nb, T // tm),
        in_specs=[pl.BlockSpec((tm, D_MODEL), lambda j, i: (i, 0)), _const_spec(g.shape),
                  wspec(0), wspec(1), wspec(2),
                  pl.BlockSpec((None, CONV_WIDTH, tn), lambda j, i: (layer, 0, j))],
        out_specs=pl.BlockSpec((tm, tn), lambda j, i: (i, j)),
        out_shape=jax.ShapeDtypeStruct((T, CONV_DIM), BF16),
        scratch_shapes=[pltpu.VMEM((8, tn), F32)],
        compiler_params=pltpu.CompilerParams(dimension_semantics=("arbitrary", "arbitrary"),
                                             vmem_limit_bytes=VMEM_LIMIT_BYTES),
        name="conv_front",
    )(x, g, w_in, w_in, w_in, conv_w)


def _even_weights(w_in, w_uq, w_ukv):
    c1 = Q_LORA
    c2 = c1 + KV_LORA
    c3 = c2 + QK_ROPE
    k1 = w_in[:, c2:c2 + HALF_ROPE]
    k2 = w_in[:, c2 + HALF_ROPE:c3]
    win = jnp.concatenate([w_in[:, :c2], w_in[:, c3:], k1, k2, k1, k2, k2, k1, k2, k1], axis=1)
    wq = w_uq.reshape(Q_LORA, MLA_HEADS, QK_NOPE + QK_ROPE)
    r1 = wq[:, :, QK_NOPE:QK_NOPE + HALF_ROPE]
    r2 = wq[:, :, QK_NOPE + HALF_ROPE:]
    pad = jnp.zeros((Q_LORA, MLA_HEADS, QK_NOPE - QK_ROPE), w_uq.dtype)
    wuq = jnp.concatenate([
        wq[:, :, :QK_NOPE].reshape(Q_LORA, -1),
        jnp.concatenate([r1, r2, pad], axis=-1).reshape(Q_LORA, -1),
        jnp.concatenate([r2, r1, pad], axis=-1).reshape(Q_LORA, -1)], axis=1)
    wkv = w_ukv.reshape(KV_LORA, MLA_HEADS, QK_NOPE + V_HEAD)
    wukv = jnp.concatenate([wkv[:, :, :QK_NOPE].reshape(KV_LORA, -1),
                            wkv[:, :, QK_NOPE:].reshape(KV_LORA, -1)], axis=1)
    return win.astype(BF16), wuq.astype(BF16), wukv.astype(BF16)


def _rope_rows():
    lane = jnp.arange(LANES)
    inv_freq = ROPE_BASE ** (-jnp.arange(0, QK_ROPE, 2, dtype=F32) / QK_ROPE)
    invf = inv_freq[lane % HALF_ROPE][None, :]
    sign = jnp.where((lane % QK_ROPE) < HALF_ROPE, -1.0, 1.0).astype(F32)[None, :]
    return invf, sign


def kernel(x, positions, e_norm_mix, e_w_in, e_q_norm, e_w_uq, e_kv_norm, e_w_ukv, e_v_norm,
           e_sgu_w, e_sgu_b, e_mla_out_norm, e_sgu_out_norm, e_w_out, o_norm_mix, o_w_in,
           o_conv_w, o_w_out, mlp_norm, mlp_w1, mlp_w2, final_norm):
    batch, seq, d = x.shape
    T = batch * seq
    depth = mlp_norm.shape[0]
    xf = x.reshape(T, d)
    pos = positions.reshape(T, 1)
    invf, sign = _rope_rows()
    fg = final_norm[None, :]
    row2d = lambda a: a.reshape(1, -1)
    w1 = mlp_w1.astype(BF16)
    w2 = mlp_w2.astype(BF16)
    e_wo = e_w_out.astype(BF16)
    o_wo = o_w_out.astype(BF16)
    o_wi = o_w_in.astype(BF16)

    for layer in range(depth):
        i = layer // 2
        if layer % 2 == 0:
            win, wuq, wukv = _even_weights(e_w_in[i], e_w_uq[i], e_w_ukv[i])
            sgub = jnp.repeat(e_sgu_b[i].T, SGU_CH, axis=1)
            q, k, v, sn = _even_front(
                xf, pos, invf, sign, row2d(e_norm_mix[i]), win, row2d(e_q_norm[i]), wuq,
                row2d(e_kv_norm[i]), wukv, row2d(e_v_norm[i]), e_sgu_w[i], sgub,
                row2d(e_sgu_out_norm[i]))
            ma = _attention(q, k, v, batch, seq)
            mb, wo, a_gain = sn, e_wo, row2d(e_mla_out_norm[i])
        else:
            ma = _conv_front(xf, row2d(o_norm_mix[i]), o_wi, o_conv_w, i, seq)
            mb, wo, a_gain = ma, o_wo, None
        xf = _out_mlp(xf, ma, mb, wo, i, row2d(mlp_norm[layer]), w1, w2, layer, fg, a_gain,
                      final_norm=(layer == depth - 1))
    return xf.reshape(batch, seq, d)
```

```python
import functools

import jax
import jax.numpy as jnp
from jax import lax
from jax.experimental import pallas as pl
from jax.experimental.pallas import tpu as pltpu

D_MODEL = 2048
MLA_HEADS = 8
Q_LORA = 512
KV_LORA = 512
QK_NOPE = 128
QK_ROPE = 64
V_HEAD = 128
ROPE_BASE = 10000.0
SGU_GROUPS = 8
SGU_CH = 128
CHUNK = 128
CONV_DIM = D_MODEL
CONV_WIDTH = 3
D_FF = 4 * D_MODEL
EPS = 1e-6
MLA_OUT = MLA_HEADS * V_HEAD
SGU_OUT = SGU_GROUPS * SGU_CH
HALF_ROPE = QK_ROPE // 2
HEAD_SLOT = 2 * QK_NOPE

LANES = 128
BF16_ROWS = 16
VMEM_LIMIT_BYTES = 60 * 1024 * 1024

FRONT_ROWS = 256
ATTN_BLOCK = 256
MLP_ROWS = 512
MLP_FF = 1024
CONV_ROWS = 512
CONV_COLS = 512

BF16 = jnp.bfloat16
F32 = jnp.float32


def _dot(a, b):
    return jnp.dot(a, b, preferred_element_type=F32)


def _dot_nt(a, b):
    return lax.dot_general(a, b, (((1,), (1,)), ((), ())), preferred_element_type=F32)


def _rms(x, g):
    return x * lax.rsqrt(jnp.mean(x * x, axis=-1, keepdims=True) + EPS) * g


def _const_spec(shape):
    return pl.BlockSpec(shape, lambda *_: (0,) * len(shape), pipeline_mode=pl.Buffered(1))


def _layer_spec(w, layer):
    return pl.BlockSpec((None,) + w.shape[1:], lambda *_: (layer, 0, 0), pipeline_mode=pl.Buffered(1))


def _cast_rider_specs(riders, steps, step_of):
    ins, outs, shapes = [], [], []
    for w, layer in riders:
        _, rows, cols = w.shape
        rb = rows // steps
        assert rb * steps == rows and rb % BF16_ROWS == 0, (w.shape, steps)
        ins.append(pl.BlockSpec((None, rb, cols), lambda *g, layer=layer: (layer, step_of(*g), 0)))
        outs.append(pl.BlockSpec((rb, cols), lambda *g: (step_of(*g), 0)))
        shapes.append(jax.ShapeDtypeStruct((rows, cols), BF16))
    return ins, outs, shapes


def _split_refs(refs, n_in, n_out, n_riders):
    cuts = [n_in, n_in + n_riders, n_in + n_riders + n_out, n_in + 2 * n_riders + n_out]
    return (refs[:cuts[0]], refs[cuts[0]:cuts[1]], refs[cuts[1]:cuts[2]], refs[cuts[2]:cuts[3]],
            refs[cuts[3]:])


def _run_cast_riders(srcs, dsts):
    for src, dst in zip(srcs, dsts):
        dst[...] = src[...].astype(BF16)


def _even_front_kernel(*refs, n_riders):
    ins, rider_in, outs, rider_out, (vn_buf, s_buf) = _split_refs(refs, 14, 4, n_riders)
    (x_ref, pos_ref, invf_ref, sign_ref, g_ref, win_ref, qg_ref, wuq_ref, kvg_ref, wukv_ref,
     vng_ref, sguw_ref, sgub_ref, sgn_ref) = ins
    q_ref, k_ref, v_ref, s_ref = outs
    _run_cast_riders(rider_in, rider_out)

    tm = x_ref.shape[0]
    scale = (QK_NOPE + QK_ROPE) ** -0.5
    h = _rms(x_ref[...], g_ref[...]).astype(BF16)

    ang = pos_ref[...].astype(F32) * invf_ref[...]
    cos = jnp.cos(ang)
    sin = jnp.sin(ang) * sign_ref[...]

    cq = _rms(_dot(h, win_ref[:, 0:Q_LORA]), qg_ref[...]).astype(BF16)
    q = _dot(cq, wuq_ref[...])
    cos_q = jnp.concatenate([cos] * MLA_HEADS, axis=1)
    sin_q = jnp.concatenate([sin] * MLA_HEADS, axis=1)
    qn = (q[:, :MLA_OUT] * scale).astype(BF16)
    qr = ((q[:, MLA_OUT:2 * MLA_OUT] * cos_q + q[:, 2 * MLA_OUT:] * sin_q) * scale).astype(BF16)

    c0 = Q_LORA
    ckv = _rms(_dot(h, win_ref[:, c0:c0 + KV_LORA]), kvg_ref[...]).astype(BF16)
    kv = _dot(ckv, wukv_ref[...])
    kn = kv[:, :MLA_OUT].astype(BF16)
    v_ref[...] = kv[:, MLA_OUT:].astype(BF16)

    c0 = Q_LORA + KV_LORA + 2 * SGU_OUT
    ka = _dot(h, win_ref[:, c0:c0 + LANES])
    kb = _dot(h, win_ref[:, c0 + LANES:c0 + 2 * LANES])
    kr = (ka * cos + kb * sin).astype(BF16)
    for hd in range(MLA_HEADS):
        lo = hd * HEAD_SLOT
        src = slice(hd * QK_NOPE, (hd + 1) * QK_NOPE)
        q_ref[:, lo:lo + QK_NOPE] = qn[:, src]
        q_ref[:, lo + QK_NOPE:lo + HEAD_SLOT] = qr[:, src]
        k_ref[:, lo:lo + QK_NOPE] = kn[:, src]
        k_ref[:, lo + QK_NOPE:lo + HEAD_SLOT] = kr

    c0 = Q_LORA + KV_LORA
    u = jax.nn.gelu(_dot(h, win_ref[:, c0:c0 + SGU_OUT]))
    vv = jax.nn.gelu(_dot(h, win_ref[:, c0 + SGU_OUT:c0 + 2 * SGU_OUT]))
    for g in range(SGU_GROUPS):
        lo = g * SGU_CH
        vg = vv[:, lo:lo + SGU_CH]
        mu = jnp.mean(vg, axis=-1, keepdims=True)
        d = vg - mu
        var = jnp.mean(d * d, axis=-1, keepdims=True)
        vn_buf[:, lo:lo + SGU_CH] = (d * lax.rsqrt(var + EPS) * vng_ref[:, lo:lo + SGU_CH]).astype(BF16)
    row = lax.broadcasted_iota(jnp.int32, (CHUNK, CHUNK), 0)
    col = lax.broadcasted_iota(jnp.int32, (CHUNK, CHUNK), 1)
    causal = col <= row
    for g in range(SGU_GROUPS):
        lo = g * SGU_CH
        w = jnp.where(causal, sguw_ref[g], 0.0).astype(BF16)
        for c in range(tm // CHUNK):
            r0 = c * CHUNK
            y = _dot(w, vn_buf[r0:r0 + CHUNK, lo:lo + SGU_CH])
            s_buf[r0:r0 + CHUNK, lo:lo + SGU_CH] = y + sgub_ref[:, lo:lo + SGU_CH]
    s_ref[...] = _rms(u * s_buf[...], sgn_ref[...]).astype(BF16)


def _even_front(x, pos, invf, sign, g, win, qg, wuq, kvg, wukv, vng, sguw, sgub, sgn, riders):
    T = x.shape[0]
    tm = FRONT_ROWS
    steps = T // tm
    row = lambda w: pl.BlockSpec((tm, w), lambda i: (i, 0))
    out_widths = (MLA_HEADS * HEAD_SLOT, MLA_HEADS * HEAD_SLOT, MLA_OUT, SGU_OUT)
    r_in, r_out, r_shapes = _cast_rider_specs(riders, steps, lambda i: i)
    consts = (invf, sign, g, win, qg, wuq, kvg, wukv, vng, sguw, sgub, sgn)
    return pl.pallas_call(
        functools.partial(_even_front_kernel, n_riders=len(riders)),
        grid=(steps,),
        in_specs=[row(D_MODEL), row(1)] + [_const_spec(c.shape) for c in consts] + r_in,
        out_specs=[row(w) for w in out_widths] + r_out,
        out_shape=[jax.ShapeDtypeStruct((T, w), BF16) for w in out_widths] + r_shapes,
        scratch_shapes=[pltpu.VMEM((tm, SGU_OUT), BF16), pltpu.VMEM((tm, SGU_OUT), F32)],
        compiler_params=pltpu.CompilerParams(dimension_semantics=("parallel",),
                                             vmem_limit_bytes=VMEM_LIMIT_BYTES),
        name="even_front",
    )(x, pos, *consts, *[w for w, _ in riders])


def _attn_kernel(*refs, n_riders):
    (q_ref, k_ref, v_ref), rider_in, (o_ref,), rider_out, _ = _split_refs(refs, 3, 1, n_riders)
    _run_cast_riders(rider_in, rider_out)
    seq = q_ref.shape[0]
    tq = ATTN_BLOCK
    row = lax.broadcasted_iota(jnp.int32, (tq, tq), 0)
    col = lax.broadcasted_iota(jnp.int32, (tq, tq), 1)
    causal = col <= row
    for qi in range(seq // tq):
        r0 = qi * tq
        r1 = r0 + tq
        q = q_ref[r0:r1, :]
        sd = jnp.where(causal, _dot_nt(q, k_ref[r0:r1, :]), -jnp.inf)
        m = jnp.max(sd, axis=-1, keepdims=True)
        if qi:
            sp = _dot_nt(q, k_ref[0:r0, :])
            m = jnp.maximum(m, jnp.max(sp, axis=-1, keepdims=True))
        pd = jnp.exp(sd - m)
        l = jnp.sum(pd, axis=-1, keepdims=True)
        acc = _dot(pd.astype(BF16), v_ref[r0:r1, :])
        if qi:
            pp = jnp.exp(sp - m)
            l = l + jnp.sum(pp, axis=-1, keepdims=True)
            acc = acc + _dot(pp.astype(BF16), v_ref[0:r0, :])
        o_ref[r0:r1, :] = (acc / l).astype(BF16)


def _attention(q, k, v, batch, seq, riders):
    T = q.shape[0]
    spec = lambda w: pl.BlockSpec((seq, w), lambda b, h: (b, h))
    r_in, r_out, r_shapes = _cast_rider_specs(riders, batch * MLA_HEADS,
                                              lambda b, h: b * MLA_HEADS + h)
    return pl.pallas_call(
        functools.partial(_attn_kernel, n_riders=len(riders)),
        grid=(batch, MLA_HEADS),
        in_specs=[spec(HEAD_SLOT), spec(HEAD_SLOT), spec(V_HEAD)] + r_in,
        out_specs=[spec(V_HEAD)] + r_out,
        out_shape=[jax.ShapeDtypeStruct((T, MLA_OUT), BF16)] + r_shapes,
        compiler_params=pltpu.CompilerParams(dimension_semantics=("parallel", "parallel"),
                                             vmem_limit_bytes=VMEM_LIMIT_BYTES),
        name="mla_attention",
    )(q, k, v, *[w for w, _ in riders])


def _out_mlp_kernel(x_ref, ma_ref, mb_ref, wo_ref, g_ref, w1_ref, w2_ref, fg_ref, *rest,
                    norm_a, final_norm):
    if norm_a:
        ag_ref, o_ref, h_buf = rest
    else:
        o_ref, h_buf = rest
    k = pl.program_id(1)
    half = ma_ref.shape[1]

    @pl.when(k == 0)
    def _():
        ma = ma_ref[...]
        if norm_a:
            ma = _rms(ma.astype(F32), ag_ref[...]).astype(BF16)
        x1 = (x_ref[...] + _dot(ma, wo_ref[0:half, :])
              + _dot(mb_ref[...], wo_ref[half:2 * half, :]))
        o_ref[...] = x1
        h_buf[...] = _rms(x1, g_ref[...]).astype(BF16)

    a = jnp.maximum(_dot(h_buf[...], w1_ref[...]), 0.0)
    o_ref[...] += _dot((a * a).astype(BF16), w2_ref[...])

    if final_norm:
        @pl.when(k == pl.num_programs(1) - 1)
        def _():
            o_ref[...] = _rms(o_ref[...], fg_ref[...])


def _out_mlp(x, ma, mb, wo, g, w1, w2, fg, a_gain, *, final_norm):
    T = x.shape[0]
    tm, tf = MLP_ROWS, MLP_FF
    half = wo.shape[0] // 2
    norm_a = a_gain is not None
    in_specs = [pl.BlockSpec((tm, D_MODEL), lambda i, k: (i, 0)),
                pl.BlockSpec((tm, half), lambda i, k: (i, 0)),
                pl.BlockSpec((tm, half), lambda i, k: (i, mb.shape[1] // half - 1)),
                _const_spec(wo.shape), _const_spec(g.shape),
                pl.BlockSpec((D_MODEL, tf), lambda i, k: (0, k)),
                pl.BlockSpec((tf, D_MODEL), lambda i, k: (k, 0)),
                _const_spec(fg.shape)]
    args = [x, ma, mb, wo, g, w1, w2, fg]
    if norm_a:
        in_specs.append(_const_spec(a_gain.shape))
        args.append(a_gain)
    return pl.pallas_call(
        functools.partial(_out_mlp_kernel, norm_a=norm_a, final_norm=final_norm),
        grid=(T // tm, D_FF // tf),
        in_specs=in_specs,
        out_specs=pl.BlockSpec((tm, D_MODEL), lambda i, k: (i, 0)),
        out_shape=jax.ShapeDtypeStruct((T, D_MODEL), F32),
        scratch_shapes=[pltpu.VMEM((tm, D_MODEL), BF16)],
        compiler_params=pltpu.CompilerParams(dimension_semantics=("parallel", "arbitrary"),
                                             vmem_limit_bytes=VMEM_LIMIT_BYTES),
        name="out_proj_mlp",
    )(*args)


def _conv_front_kernel(x_ref, g_ref, w_ref, cw_ref, o_ref, hist, *, tiles_per_seq):
    tm = x_ref.shape[0]
    tn = CONV_COLS
    i = pl.program_id(0)
    h = _rms(x_ref[...], g_ref[...]).astype(BF16)

    @pl.when(i % tiles_per_seq == 0)
    def _():
        hist[...] = jnp.zeros_like(hist)

    row = lax.broadcasted_iota(jnp.int32, (tm, tn), 0)
    for cb in range(CONV_DIM // tn):
        c0 = cb * tn
        z = (_dot(h, w_ref[:, CONV_DIM + c0:CONV_DIM + c0 + tn])
             * _dot(h, w_ref[:, 2 * CONV_DIM + c0:2 * CONV_DIM + c0 + tn]))
        p1 = hist[7:8, c0:c0 + tn]
        p2 = hist[6:7, c0:c0 + tn]
        z1 = jnp.where(row == 0, p1, pltpu.roll(z, 1, 0))
        z2 = jnp.where(row == 0, p2, jnp.where(row == 1, p1, pltpu.roll(z, 2, 0)))
        conv = (cw_ref[0:1, c0:c0 + tn] * z2 + cw_ref[1:2, c0:c0 + tn] * z1
                + cw_ref[2:3, c0:c0 + tn] * z)
        o_ref[:, c0:c0 + tn] = (_dot(h, w_ref[:, c0:c0 + tn]) * conv).astype(BF16)
        hist[:, c0:c0 + tn] = z[tm - 8:tm, :]


def _conv_front(x, g, w_in, conv_w, layer, seq):
    T = x.shape[0]
    tm = CONV_ROWS
    return pl.pallas_call(
        functools.partial(_conv_front_kernel, tiles_per_seq=seq // tm),
        grid=(T // tm,),
        in_specs=[pl.BlockSpec((tm, D_MODEL), lambda i: (i, 0)), _const_spec(g.shape),
                  _const_spec(w_in.shape), _layer_spec(conv_w, layer)],
        out_specs=pl.BlockSpec((tm, CONV_DIM), lambda i: (i, 0)),
        out_shape=jax.ShapeDtypeStruct((T, CONV_DIM), BF16),
        scratch_shapes=[pltpu.VMEM((8, CONV_DIM), F32)],
        compiler_params=pltpu.CompilerParams(dimension_semantics=("arbitrary",),
                                             vmem_limit_bytes=VMEM_LIMIT_BYTES),
        name="conv_front",
    )(x, g, w_in, conv_w)


def _even_weights(w_in, w_uq, w_ukv):
    c1 = Q_LORA
    c2 = c1 + KV_LORA
    c3 = c2 + QK_ROPE
    k1 = w_in[:, c2:c2 + HALF_ROPE]
    k2 = w_in[:, c2 + HALF_ROPE:c3]
    win = jnp.concatenate([w_in[:, :c2], w_in[:, c3:], k1, k2, k1, k2, k2, k1, k2, k1], axis=1)
    wq = w_uq.reshape(Q_LORA, MLA_HEADS, QK_NOPE + QK_ROPE)
    r1 = wq[:, :, QK_NOPE:QK_NOPE + HALF_ROPE]
    r2 = wq[:, :, QK_NOPE + HALF_ROPE:]
    pad = jnp.zeros((Q_LORA, MLA_HEADS, QK_NOPE - QK_ROPE), w_uq.dtype)
    wuq = jnp.concatenate([
        wq[:, :, :QK_NOPE].reshape(Q_LORA, -1),
        jnp.concatenate([r1, r2, pad], axis=-1).reshape(Q_LORA, -1),
        jnp.concatenate([r2, r1, pad], axis=-1).reshape(Q_LORA, -1)], axis=1)
    wkv = w_ukv.reshape(KV_LORA, MLA_HEADS, QK_NOPE + V_HEAD)
    wukv = jnp.concatenate([wkv[:, :, :QK_NOPE].reshape(KV_LORA, -1),
                            wkv[:, :, QK_NOPE:].reshape(KV_LORA, -1)], axis=1)
    return win.astype(BF16), wuq.astype(BF16), wukv.astype(BF16)


def _rope_rows():
    lane = jnp.arange(LANES)
    inv_freq = ROPE_BASE ** (-jnp.arange(0, QK_ROPE, 2, dtype=F32) / QK_ROPE)
    invf = inv_freq[lane % HALF_ROPE][None, :]
    sign = jnp.where((lane % QK_ROPE) < HALF_ROPE, -1.0, 1.0).astype(F32)[None, :]
    return invf, sign


def kernel(x, positions, e_norm_mix, e_w_in, e_q_norm, e_w_uq, e_kv_norm, e_w_ukv, e_v_norm,
           e_sgu_w, e_sgu_b, e_mla_out_norm, e_sgu_out_norm, e_w_out, o_norm_mix, o_w_in,
           o_conv_w, o_w_out, mlp_norm, mlp_w1, mlp_w2, final_norm):
    batch, seq, d = x.shape
    T = batch * seq
    depth = mlp_norm.shape[0]
    xf = x.reshape(T, d)
    pos = positions.reshape(T, 1)
    invf, sign = _rope_rows()
    fg = final_norm[None, :]
    row2d = lambda a: a.reshape(1, -1)

    odd_weights = None
    for layer in range(depth):
        i = layer // 2
        if layer % 2 == 0:
            win, wuq, wukv = _even_weights(e_w_in[i], e_w_uq[i], e_w_ukv[i])
            sgub = jnp.repeat(e_sgu_b[i].T, SGU_CH, axis=1)
            q, k, v, sn, wo, w1, w2 = _even_front(
                xf, pos, invf, sign, row2d(e_norm_mix[i]), win, row2d(e_q_norm[i]), wuq,
                row2d(e_kv_norm[i]), wukv, row2d(e_v_norm[i]), e_sgu_w[i], sgub,
                row2d(e_sgu_out_norm[i]),
                riders=[(e_w_out, i), (mlp_w1, layer), (mlp_w2, layer)])
            riders = []
            if layer + 1 < depth:
                j = (layer + 1) // 2
                riders = [(o_w_in, j), (o_w_out, j), (mlp_w1, layer + 1), (mlp_w2, layer + 1)]
            ma, *odd_weights = _attention(q, k, v, batch, seq, riders)
            mb, a_gain = sn, row2d(e_mla_out_norm[i])
        else:
            o_wi, wo, w1, w2 = odd_weights
            ma = _conv_front(xf, row2d(o_norm_mix[i]), o_wi, o_conv_w, i, seq)
            mb, a_gain = ma, None
        xf = _out_mlp(xf, ma, mb, wo, row2d(mlp_norm[layer]), w1, w2, fg, a_gain,
                      final_norm=(layer == depth - 1))
    return xf.reshape(batch, seq, d)
```

```python
import functools

import jax
import jax.numpy as jnp
from jax import lax
from jax.experimental import pallas as pl
from jax.experimental.pallas import tpu as pltpu

D_MODEL = 2048
MLA_HEADS = 8
Q_LORA = 512
KV_LORA = 512
QK_NOPE = 128
QK_ROPE = 64
V_HEAD = 128
ROPE_BASE = 10000.0
SGU_GROUPS = 8
SGU_CH = 128
CHUNK = 128
CONV_DIM = D_MODEL
CONV_WIDTH = 3
D_FF = 4 * D_MODEL
EPS = 1e-6
MLA_OUT = MLA_HEADS * V_HEAD
SGU_OUT = SGU_GROUPS * SGU_CH
HALF_ROPE = QK_ROPE // 2
HEAD_SLOT = 2 * QK_NOPE

LANES = 128
BF16_ROWS = 16
VMEM_LIMIT_BYTES = 60 * 1024 * 1024

FRONT_ROWS = 512
FRONT_SUB = 256
ROPE_TABLE_ROWS = 512
ATTN_BLOCK = 256
MLP_ROWS = 512
MLP_FF = 1024
CONV_ROWS = 512
CONV_COLS = 512

BF16 = jnp.bfloat16
F32 = jnp.float32


def _dot(a, b):
    return jnp.dot(a, b, preferred_element_type=F32)


def _dot_nt(a, b):
    return lax.dot_general(a, b, (((1,), (1,)), ((), ())), preferred_element_type=F32)


def _rms(x, g):
    return x * lax.rsqrt(jnp.mean(x * x, axis=-1, keepdims=True) + EPS) * g


def _const_spec(shape):
    return pl.BlockSpec(shape, lambda *_: (0,) * len(shape), pipeline_mode=pl.Buffered(1))


def _layer_spec(w, layer):
    return pl.BlockSpec((None,) + w.shape[1:], lambda *_: (layer, 0, 0), pipeline_mode=pl.Buffered(1))


def _cast_rider_specs(riders, steps, step_of):
    ins, outs, shapes = [], [], []
    for w, layer in riders:
        _, rows, cols = w.shape
        rb = rows // steps
        assert rb * steps == rows and rb % BF16_ROWS == 0, (w.shape, steps)
        ins.append(pl.BlockSpec((None, rb, cols), lambda *g, layer=layer: (layer, step_of(*g), 0)))
        outs.append(pl.BlockSpec((rb, cols), lambda *g: (step_of(*g), 0)))
        shapes.append(jax.ShapeDtypeStruct((rows, cols), BF16))
    return ins, outs, shapes


def _split_refs(refs, n_in, n_out, n_riders):
    cuts = [n_in, n_in + n_riders, n_in + n_riders + n_out, n_in + 2 * n_riders + n_out]
    return (refs[:cuts[0]], refs[cuts[0]:cuts[1]], refs[cuts[1]:cuts[2]], refs[cuts[2]:cuts[3]],
            refs[cuts[3]:])


def _run_cast_riders(srcs, dsts):
    for src, dst in zip(srcs, dsts):
        dst[...] = src[...].astype(BF16)


def _even_front_kernel(*refs, n_riders):
    ins, rider_in, outs, rider_out, (vn_buf, s_buf, sw_buf) = _split_refs(refs, 13, 4, n_riders)
    (x_ref, cos_ref, sin_ref, g_ref, win_ref, qg_ref, wuq_ref, kvg_ref, wukv_ref,
     vng_ref, sguw_ref, sgub_ref, sgn_ref) = ins
    q_ref, k_ref, v_ref, s_ref = outs
    _run_cast_riders(rider_in, rider_out)

    @pl.when(pl.program_id(0) == 0)
    def _():
        row = lax.broadcasted_iota(jnp.int32, (CHUNK, CHUNK), 0)
        col = lax.broadcasted_iota(jnp.int32, (CHUNK, CHUNK), 1)
        for g in range(SGU_GROUPS):
            sw_buf[g] = jnp.where(col <= row, sguw_ref[g], 0.0).astype(BF16)

    scale = (QK_NOPE + QK_ROPE) ** -0.5
    n_rope_cols = MLA_HEADS * QK_ROPE // LANES
    low_lanes = lax.broadcasted_iota(jnp.int32, (FRONT_SUB, LANES), 1) < QK_ROPE

    for sb in range(x_ref.shape[0] // FRONT_SUB):
        rows = slice(sb * FRONT_SUB, (sb + 1) * FRONT_SUB)
        h = _rms(x_ref[rows, :], g_ref[...]).astype(BF16)
        cos = cos_ref[rows, :]
        sin = sin_ref[rows, :]

        cq = _rms(_dot(h, win_ref[:, 0:Q_LORA]), qg_ref[...]).astype(BF16)
        q = _dot(cq, wuq_ref[...])
        cos_q = jnp.concatenate([cos] * n_rope_cols, axis=1)
        sin_q = jnp.concatenate([sin] * n_rope_cols, axis=1)
        r0 = MLA_OUT
        r1 = r0 + n_rope_cols * LANES
        qn = (q[:, :MLA_OUT] * scale).astype(BF16)
        qr = ((q[:, r0:r1] * cos_q + q[:, r1:] * sin_q) * scale).astype(BF16)

        c0 = Q_LORA
        ckv = _rms(_dot(h, win_ref[:, c0:c0 + KV_LORA]), kvg_ref[...]).astype(BF16)
        kv = _dot(ckv, wukv_ref[...])
        kn = kv[:, :MLA_OUT].astype(BF16)
        v_ref[rows, :] = kv[:, MLA_OUT:].astype(BF16)

        c0 = Q_LORA + KV_LORA + 2 * SGU_OUT
        ka = _dot(h, win_ref[:, c0:c0 + LANES])
        kb = _dot(h, win_ref[:, c0 + LANES:c0 + 2 * LANES])
        kr = (ka * cos + kb * sin).astype(BF16)
        for hd in range(MLA_HEADS):
            lo = hd * HEAD_SLOT
            src = slice(hd * QK_NOPE, (hd + 1) * QK_NOPE)
            pair = hd % n_rope_cols
            rope = qr[:, pair * LANES:(pair + 1) * LANES]
            keep = low_lanes if hd < n_rope_cols else ~low_lanes
            q_ref[rows, lo:lo + QK_NOPE] = qn[:, src]
            q_ref[rows, lo + QK_NOPE:lo + HEAD_SLOT] = jnp.where(keep, rope, jnp.zeros_like(rope))
            k_ref[rows, lo:lo + QK_NOPE] = kn[:, src]
            k_ref[rows, lo + QK_NOPE:lo + HEAD_SLOT] = kr

        c0 = Q_LORA + KV_LORA
        u = jax.nn.gelu(_dot(h, win_ref[:, c0:c0 + SGU_OUT]))
        vv = jax.nn.gelu(_dot(h, win_ref[:, c0 + SGU_OUT:c0 + 2 * SGU_OUT]))
        for g in range(SGU_GROUPS):
            lo = g * SGU_CH
            vg = vv[:, lo:lo + SGU_CH]
            mu = jnp.mean(vg, axis=-1, keepdims=True)
            d = vg - mu
            var = jnp.mean(d * d, axis=-1, keepdims=True)
            vn_buf[rows, lo:lo + SGU_CH] = (d * lax.rsqrt(var + EPS)
                                            * vng_ref[:, lo:lo + SGU_CH]).astype(BF16)
        for g in range(SGU_GROUPS):
            lo = g * SGU_CH
            for c in range(FRONT_SUB // CHUNK):
                t0 = sb * FRONT_SUB + c * CHUNK
                y = _dot(sw_buf[g], vn_buf[t0:t0 + CHUNK, lo:lo + SGU_CH])
                s_buf[t0:t0 + CHUNK, lo:lo + SGU_CH] = y + sgub_ref[:, lo:lo + SGU_CH]
        s_ref[rows, :] = _rms(u * s_buf[rows, :], sgn_ref[...]).astype(BF16)


def _even_front(x, cos, sin, g, win, qg, wuq, kvg, wukv, vng, sguw, sgub, sgn, riders):
    T = x.shape[0]
    tm = FRONT_ROWS
    steps = T // tm
    row = lambda w: pl.BlockSpec((tm, w), lambda i: (i, 0))
    out_widths = (MLA_HEADS * HEAD_SLOT, MLA_HEADS * HEAD_SLOT, MLA_OUT, SGU_OUT)
    r_in, r_out, r_shapes = _cast_rider_specs(riders, steps, lambda i: i)
    consts = (g, win, qg, wuq, kvg, wukv, vng, sguw, sgub, sgn)
    return pl.pallas_call(
        functools.partial(_even_front_kernel, n_riders=len(riders)),
        grid=(steps,),
        in_specs=([row(D_MODEL), row(LANES), row(LANES)]
                  + [_const_spec(c.shape) for c in consts] + r_in),
        out_specs=[row(w) for w in out_widths] + r_out,
        out_shape=[jax.ShapeDtypeStruct((T, w), BF16) for w in out_widths] + r_shapes,
        scratch_shapes=[pltpu.VMEM((tm, SGU_OUT), BF16), pltpu.VMEM((tm, SGU_OUT), F32),
                        pltpu.VMEM((SGU_GROUPS, CHUNK, CHUNK), BF16)],
        compiler_params=pltpu.CompilerParams(dimension_semantics=("arbitrary",),
                                             vmem_limit_bytes=VMEM_LIMIT_BYTES),
        name="even_front",
    )(x, cos, sin, *consts, *[w for w, _ in riders])


def _rope_table_kernel(pos_ref, invf_ref, cos_ref, sin_ref):
    ang = pos_ref[...].astype(F32) * invf_ref[...]
    cos_ref[...] = jnp.cos(ang)
    sin_ref[...] = jnp.sin(ang)


def _rope_tables(positions):
    T = positions.size
    per_row = LANES // HALF_ROPE
    rows = T // per_row
    tr = min(rows, ROPE_TABLE_ROWS)
    inv_freq = ROPE_BASE ** (-jnp.arange(0, QK_ROPE, 2, dtype=F32) / QK_ROPE)
    invf = jnp.tile(inv_freq, per_row)[None, :]
    pos = jnp.repeat(positions.reshape(rows, per_row), HALF_ROPE, axis=1)
    spec = pl.BlockSpec((tr, LANES), lambda i: (i, 0))
    cos, sin = pl.pallas_call(
        _rope_table_kernel,
        grid=(rows // tr,),
        in_specs=[spec, _const_spec(invf.shape)],
        out_specs=[spec, spec],
        out_shape=[jax.ShapeDtypeStruct((rows, LANES), F32)] * 2,
        compiler_params=pltpu.CompilerParams(dimension_semantics=("parallel",)),
        name="rope_tables",
    )(pos, invf)
    lane = jnp.arange(LANES)
    sign = jnp.where((lane % QK_ROPE) < HALF_ROPE, -1.0, 1.0).astype(F32)
    cos = jnp.tile(cos.reshape(T, HALF_ROPE), (1, per_row))
    sin = jnp.tile(sin.reshape(T, HALF_ROPE), (1, per_row)) * sign
    return cos, sin


def _attn_kernel(*refs, n_riders):
    (q_ref, k_ref, v_ref), rider_in, (o_ref,), rider_out, _ = _split_refs(refs, 3, 1, n_riders)
    _run_cast_riders(rider_in, rider_out)
    seq = q_ref.shape[0]
    tq = ATTN_BLOCK
    row = lax.broadcasted_iota(jnp.int32, (tq, tq), 0)
    col = lax.broadcasted_iota(jnp.int32, (tq, tq), 1)
    causal = col <= row
    for qi in range(seq // tq):
        r0 = qi * tq
        r1 = r0 + tq
        q = q_ref[r0:r1, :]
        sd = jnp.where(causal, _dot_nt(q, k_ref[r0:r1, :]), -jnp.inf)
        m = jnp.max(sd, axis=-1, keepdims=True)
        if qi:
            sp = _dot_nt(q, k_ref[0:r0, :])
            m = jnp.maximum(m, jnp.max(sp, axis=-1, keepdims=True))
        pd = jnp.exp(sd - m)
        l = jnp.sum(pd, axis=-1, keepdims=True)
        acc = _dot(pd.astype(BF16), v_ref[r0:r1, :])
        if qi:
            pp = jnp.exp(sp - m)
            l = l + jnp.sum(pp, axis=-1, keepdims=True)
            acc = acc + _dot(pp.astype(BF16), v_ref[0:r0, :])
        o_ref[r0:r1, :] = (acc / l).astype(BF16)


def _attention(q, k, v, batch, seq, riders):
    T = q.shape[0]
    spec = lambda w: pl.BlockSpec((seq, w), lambda b, h: (b, h))
    r_in, r_out, r_shapes = _cast_rider_specs(riders, batch * MLA_HEADS,
                                              lambda b, h: b * MLA_HEADS + h)
    return pl.pallas_call(
        functools.partial(_attn_kernel, n_riders=len(riders)),
        grid=(batch, MLA_HEADS),
        in_specs=[spec(HEAD_SLOT), spec(HEAD_SLOT), spec(V_HEAD)] + r_in,
        out_specs=[spec(V_HEAD)] + r_out,
        out_shape=[jax.ShapeDtypeStruct((T, MLA_OUT), BF16)] + r_shapes,
        compiler_params=pltpu.CompilerParams(dimension_semantics=("parallel", "parallel"),
                                             vmem_limit_bytes=VMEM_LIMIT_BYTES),
        name="mla_attention",
    )(q, k, v, *[w for w, _ in riders])


def _out_mlp_kernel(x_ref, ma_ref, mb_ref, wo_ref, g_ref, w1_ref, w2_ref, fg_ref, *rest,
                    norm_a, final_norm):
    if norm_a:
        ag_ref, o_ref, h_buf = rest
    else:
        o_ref, h_buf = rest
    k = pl.program_id(1)
    half = ma_ref.shape[1]

    @pl.when(k == 0)
    def _():
        ma = ma_ref[...]
        if norm_a:
            ma = _rms(ma.astype(F32), ag_ref[...]).astype(BF16)
        x1 = (x_ref[...] + _dot(ma, wo_ref[0:half, :])
              + _dot(mb_ref[...], wo_ref[half:2 * half, :]))
        o_ref[...] = x1
        h_buf[...] = _rms(x1, g_ref[...]).astype(BF16)

    a = jnp.maximum(_dot(h_buf[...], w1_ref[...]), 0.0)
    o_ref[...] += _dot((a * a).astype(BF16), w2_ref[...])

    if final_norm:
        @pl.when(k == pl.num_programs(1) - 1)
        def _():
            o_ref[...] = _rms(o_ref[...], fg_ref[...])


def _out_mlp(x, ma, mb, wo, g, w1, w2, fg, a_gain, *, final_norm):
    T = x.shape[0]
    tm, tf = MLP_ROWS, MLP_FF
    half = wo.shape[0] // 2
    norm_a = a_gain is not None
    in_specs = [pl.BlockSpec((tm, D_MODEL), lambda i, k: (i, 0)),
                pl.BlockSpec((tm, half), lambda i, k: (i, 0)),
                pl.BlockSpec((tm, half), lambda i, k: (i, mb.shape[1] // half - 1)),
                _const_spec(wo.shape), _const_spec(g.shape),
                pl.BlockSpec((D_MODEL, tf), lambda i, k: (0, k)),
                pl.BlockSpec((tf, D_MODEL), lambda i, k: (k, 0)),
                _const_spec(fg.shape)]
    args = [x, ma, mb, wo, g, w1, w2, fg]
    if norm_a:
        in_specs.append(_const_spec(a_gain.shape))
        args.append(a_gain)
    return pl.pallas_call(
        functools.partial(_out_mlp_kernel, norm_a=norm_a, final_norm=final_norm),
        grid=(T // tm, D_FF // tf),
        in_specs=in_specs,
        out_specs=pl.BlockSpec((tm, D_MODEL), lambda i, k: (i, 0)),
        out_shape=jax.ShapeDtypeStruct((T, D_MODEL), F32),
        scratch_shapes=[pltpu.VMEM((tm, D_MODEL), BF16)],
        compiler_params=pltpu.CompilerParams(dimension_semantics=("parallel", "arbitrary"),
                                             vmem_limit_bytes=VMEM_LIMIT_BYTES),
        name="out_proj_mlp",
    )(*args)


def _conv_front_kernel(*refs, n_riders, tiles_per_seq):
    ins, rider_in, (o_ref,), rider_out, (hist,) = _split_refs(refs, 4, 1, n_riders)
    x_ref, g_ref, w_ref, cw_ref = ins
    _run_cast_riders(rider_in, rider_out)
    tm = x_ref.shape[0]
    tn = CONV_COLS
    i = pl.program_id(0)
    h = _rms(x_ref[...], g_ref[...]).astype(BF16)

    @pl.when(i % tiles_per_seq == 0)
    def _():
        hist[...] = jnp.zeros_like(hist)

    row = lax.broadcasted_iota(jnp.int32, (tm, tn), 0)
    for cb in range(CONV_DIM // tn):
        c0 = cb * tn
        z = (_dot(h, w_ref[:, CONV_DIM + c0:CONV_DIM + c0 + tn])
             * _dot(h, w_ref[:, 2 * CONV_DIM + c0:2 * CONV_DIM + c0 + tn]))
        p1 = hist[7:8, c0:c0 + tn]
        p2 = hist[6:7, c0:c0 + tn]
        z1 = jnp.where(row == 0, p1, pltpu.roll(z, 1, 0))
        z2 = jnp.where(row == 0, p2, jnp.where(row == 1, p1, pltpu.roll(z, 2, 0)))
        conv = (cw_ref[0:1, c0:c0 + tn] * z2 + cw_ref[1:2, c0:c0 + tn] * z1
                + cw_ref[2:3, c0:c0 + tn] * z)
        o_ref[:, c0:c0 + tn] = (_dot(h, w_ref[:, c0:c0 + tn]) * conv).astype(BF16)
        hist[:, c0:c0 + tn] = z[tm - 8:tm, :]


def _conv_front(x, g, w_in, conv_w, layer, seq, riders):
    T = x.shape[0]
    tm = CONV_ROWS
    steps = T // tm
    r_in, r_out, r_shapes = _cast_rider_specs(riders, steps, lambda i: i)
    return pl.pallas_call(
        functools.partial(_conv_front_kernel, n_riders=len(riders), tiles_per_seq=seq // tm),
        grid=(steps,),
        in_specs=[pl.BlockSpec((tm, D_MODEL), lambda i: (i, 0)), _const_spec(g.shape),
                  _const_spec(w_in.shape), _layer_spec(conv_w, layer)] + r_in,
        out_specs=[pl.BlockSpec((tm, CONV_DIM), lambda i: (i, 0))] + r_out,
        out_shape=[jax.ShapeDtypeStruct((T, CONV_DIM), BF16)] + r_shapes,
        scratch_shapes=[pltpu.VMEM((8, CONV_DIM), F32)],
        compiler_params=pltpu.CompilerParams(dimension_semantics=("arbitrary",),
                                             vmem_limit_bytes=VMEM_LIMIT_BYTES),
        name="conv_front",
    )(x, g, w_in, conv_w, *[w for w, _ in riders])


def _even_weights(w_in, w_uq, w_ukv):
    c1 = Q_LORA
    c2 = c1 + KV_LORA
    c3 = c2 + QK_ROPE
    k1 = w_in[:, c2:c2 + HALF_ROPE]
    k2 = w_in[:, c2 + HALF_ROPE:c3]
    win = jnp.concatenate([w_in[:, :c2], w_in[:, c3:], k1, k2, k1, k2, k2, k1, k2, k1], axis=1)
    wq = w_uq.reshape(Q_LORA, MLA_HEADS, QK_NOPE + QK_ROPE)
    r1 = wq[:, :, QK_NOPE:QK_NOPE + HALF_ROPE]
    r2 = wq[:, :, QK_NOPE + HALF_ROPE:]
    pairs = LANES // QK_ROPE

    def rope_cols(a, b):
        w = jnp.concatenate([a, b], axis=-1).reshape(Q_LORA, pairs, MLA_HEADS // pairs, QK_ROPE)
        return w.transpose(0, 2, 1, 3).reshape(Q_LORA, -1)

    wuq = jnp.concatenate([wq[:, :, :QK_NOPE].reshape(Q_LORA, -1), rope_cols(r1, r2),
                           rope_cols(r2, r1)], axis=1)
    wkv = w_ukv.reshape(KV_LORA, MLA_HEADS, QK_NOPE + V_HEAD)
    wukv = jnp.concatenate([wkv[:, :, :QK_NOPE].reshape(KV_LORA, -1),
                            wkv[:, :, QK_NOPE:].reshape(KV_LORA, -1)], axis=1)
    return win.astype(BF16), wuq.astype(BF16), wukv.astype(BF16)


def kernel(x, positions, e_norm_mix, e_w_in, e_q_norm, e_w_uq, e_kv_norm, e_w_ukv, e_v_norm,
           e_sgu_w, e_sgu_b, e_mla_out_norm, e_sgu_out_norm, e_w_out, o_norm_mix, o_w_in,
           o_conv_w, o_w_out, mlp_norm, mlp_w1, mlp_w2, final_norm):
    batch, seq, d = x.shape
    T = batch * seq
    depth = mlp_norm.shape[0]
    xf = x.reshape(T, d)
    cos, sin = _rope_tables(positions)
    fg = final_norm[None, :]
    row2d = lambda a: a.reshape(1, -1)

    o_wi = None
    for layer in range(depth):
        i = layer // 2
        mlp_riders = [(mlp_w1, layer), (mlp_w2, layer)]
        if layer % 2 == 0:
            win, wuq, wukv = _even_weights(e_w_in[i], e_w_uq[i], e_w_ukv[i])
            sgub = jnp.repeat(e_sgu_b[i].T, SGU_CH, axis=1)
            q, k, v, sn, wo = _even_front(
                xf, cos, sin, row2d(e_norm_mix[i]), win, row2d(e_q_norm[i]), wuq,
                row2d(e_kv_norm[i]), wukv, row2d(e_v_norm[i]), e_sgu_w[i], sgub,
                row2d(e_sgu_out_norm[i]), riders=[(e_w_out, i)])
            next_in = [(o_w_in, (layer + 1) // 2)] if layer + 1 < depth else []
            ma, w1, w2, *rest = _attention(q, k, v, batch, seq, mlp_riders + next_in)
            o_wi = rest[0] if rest else None
            mb, a_gain = sn, row2d(e_mla_out_norm[i])
        else:
            ma, wo, w1, w2 = _conv_front(xf, row2d(o_norm_mix[i]), o_wi, o_conv_w, i, seq,
                                         riders=[(o_w_out, i)] + mlp_riders)
            mb, a_gain = ma, None
        xf = _out_mlp(xf, ma, mb, wo, row2d(mlp_norm[layer]), w1, w2, fg, a_gain,
                      final_norm=(layer == depth - 1))
    return xf.reshape(batch, seq, d)
```

```python
import functools

import jax
import jax.numpy as jnp
from jax import lax
from jax.experimental import pallas as pl
from jax.experimental.pallas import tpu as pltpu

D_MODEL = 2048
MLA_HEADS = 8
Q_LORA = 512
KV_LORA = 512
QK_NOPE = 128
QK_ROPE = 64
V_HEAD = 128
ROPE_BASE = 10000.0
SGU_GROUPS = 8
SGU_CH = 128
CHUNK = 128
CONV_DIM = D_MODEL
CONV_WIDTH = 3
D_FF = 4 * D_MODEL
EPS = 1e-6
MLA_OUT = MLA_HEADS * V_HEAD
SGU_OUT = SGU_GROUPS * SGU_CH
HALF_ROPE = QK_ROPE // 2
HEAD_SLOT = 2 * QK_NOPE

LANES = 128
BF16_ROWS = 16
VMEM_LIMIT_BYTES = 60 * 1024 * 1024

FRONT_ROWS = 512
FRONT_SUB = 256
ATTN_BLOCK = 256
MLP_ROWS = 512
MLP_SUB = 256
MLP_FF = 1024
CONV_ROWS = 512
CONV_COLS = 512

BF16 = jnp.bfloat16
F32 = jnp.float32


def _dot(a, b):
    return jnp.dot(a, b, preferred_element_type=F32)


def _dot_nt(a, b):
    return lax.dot_general(a, b, (((1,), (1,)), ((), ())), preferred_element_type=F32)


def _rms(x, g):
    return x * lax.rsqrt(jnp.mean(x * x, axis=-1, keepdims=True) + EPS) * g


def _const_spec(shape):
    return pl.BlockSpec(shape, lambda *_: (0,) * len(shape), pipeline_mode=pl.Buffered(1))


def _layer_spec(w, layer):
    return pl.BlockSpec((None,) + w.shape[1:], lambda *_: (layer, 0, 0), pipeline_mode=pl.Buffered(1))


def _cast_rider_specs(riders, steps, step_of):
    ins, outs, shapes = [], [], []
    for w, layer in riders:
        _, rows, cols = w.shape
        rb = rows // steps
        assert rb * steps == rows and rb % BF16_ROWS == 0, (w.shape, steps)
        ins.append(pl.BlockSpec((None, rb, cols), lambda *g, layer=layer: (layer, step_of(*g), 0)))
        outs.append(pl.BlockSpec((rb, cols), lambda *g: (step_of(*g), 0)))
        shapes.append(jax.ShapeDtypeStruct((rows, cols), BF16))
    return ins, outs, shapes


def _split_refs(refs, n_in, n_out, n_riders):
    cuts = [n_in, n_in + n_riders, n_in + n_riders + n_out, n_in + 2 * n_riders + n_out]
    return (refs[:cuts[0]], refs[cuts[0]:cuts[1]], refs[cuts[1]:cuts[2]], refs[cuts[2]:cuts[3]],
            refs[cuts[3]:])


def _run_cast_riders(srcs, dsts):
    for src, dst in zip(srcs, dsts):
        dst[...] = src[...].astype(BF16)


def _rope_tables(pos_ref, r0, invf, lane_group):
    n = LANES // HALF_ROPE
    seg = lane_group.shape[0]
    packed = jnp.zeros((seg, LANES), F32)
    for j in range(n):
        p = pos_ref[r0 + j * seg:r0 + (j + 1) * seg, :].astype(F32)
        packed = jnp.where(lane_group == j, p, packed)
    ang = packed * invf
    tables = []
    for t in (jnp.cos(ang), jnp.sin(ang)):
        parts = []
        for j in range(n):
            one = jnp.where(lane_group == j, t, 0.0)
            full = one
            for s in range(1, n):
                full = full + pltpu.roll(one, s * HALF_ROPE, 1)
            parts.append(full)
        tables.append(jnp.concatenate(parts, axis=0))
    return tables


def _even_front_kernel(*refs, n_riders):
    ins, rider_in, outs, rider_out, (vn_buf, s_buf, sw_buf) = _split_refs(refs, 16, 4, n_riders)
    (x_ref, pos_ref, invf_ref, sign_ref, g_ref, wlat_ref, wuv_ref, wkr_ref, qg_ref, wuq_ref,
     kvg_ref, wukv_ref, vng_ref, sguw_ref, sgub_ref, sgn_ref) = ins
    q_ref, k_ref, v_ref, s_ref = outs
    _run_cast_riders(rider_in, rider_out)

    @pl.when(pl.program_id(0) == 0)
    def _():
        row = lax.broadcasted_iota(jnp.int32, (CHUNK, CHUNK), 0)
        col = lax.broadcasted_iota(jnp.int32, (CHUNK, CHUNK), 1)
        for g in range(SGU_GROUPS):
            sw_buf[g] = jnp.where(col <= row, sguw_ref[g], 0.0).astype(BF16)

    scale = (QK_NOPE + QK_ROPE) ** -0.5
    n_rope_cols = MLA_HEADS * QK_ROPE // LANES
    low_lanes = lax.broadcasted_iota(jnp.int32, (FRONT_SUB, LANES), 1) < QK_ROPE
    seg = FRONT_SUB // (LANES // HALF_ROPE)
    lane_group = lax.broadcasted_iota(jnp.int32, (seg, LANES), 1) // HALF_ROPE

    for sb in range(x_ref.shape[0] // FRONT_SUB):
        rows = slice(sb * FRONT_SUB, (sb + 1) * FRONT_SUB)
        h = _rms(x_ref[rows, :], g_ref[...]).astype(BF16)
        cos, sin = _rope_tables(pos_ref, sb * FRONT_SUB, invf_ref[...], lane_group)
        sin = sin * sign_ref[...]

        cq = _rms(_dot(h, wlat_ref[:, 0:Q_LORA]), qg_ref[...]).astype(BF16)
        q = _dot(cq, wuq_ref[...])
        cos_q = jnp.concatenate([cos] * n_rope_cols, axis=1)
        sin_q = jnp.concatenate([sin] * n_rope_cols, axis=1)
        r0 = MLA_OUT
        r1 = r0 + n_rope_cols * LANES
        qn = (q[:, :MLA_OUT] * scale).astype(BF16)
        qr = ((q[:, r0:r1] * cos_q + q[:, r1:] * sin_q) * scale).astype(BF16)

        ckv = _rms(_dot(h, wlat_ref[:, Q_LORA:Q_LORA + KV_LORA]), kvg_ref[...]).astype(BF16)
        kv = _dot(ckv, wukv_ref[...])
        kn = kv[:, :MLA_OUT].astype(BF16)
        v_ref[rows, :] = kv[:, MLA_OUT:].astype(BF16)

        ka = _dot(h, wkr_ref[:, 0:LANES])
        kb = _dot(h, wkr_ref[:, LANES:2 * LANES])
        kr = (ka * cos + kb * sin).astype(BF16)
        for hd in range(MLA_HEADS):
            lo = hd * HEAD_SLOT
            src = slice(hd * QK_NOPE, (hd + 1) * QK_NOPE)
            pair = hd % n_rope_cols
            rope = qr[:, pair * LANES:(pair + 1) * LANES]
            keep = low_lanes if hd < n_rope_cols else ~low_lanes
            q_ref[rows, lo:lo + QK_NOPE] = qn[:, src]
            q_ref[rows, lo + QK_NOPE:lo + HEAD_SLOT] = jnp.where(keep, rope, jnp.zeros_like(rope))
            k_ref[rows, lo:lo + QK_NOPE] = kn[:, src]
            k_ref[rows, lo + QK_NOPE:lo + HEAD_SLOT] = kr

        u = jax.nn.gelu(_dot(h, wuv_ref[:, 0:SGU_OUT]))
        vv = jax.nn.gelu(_dot(h, wuv_ref[:, SGU_OUT:2 * SGU_OUT]))
        for g in range(SGU_GROUPS):
            lo = g * SGU_CH
            vg = vv[:, lo:lo + SGU_CH]
            mu = jnp.mean(vg, axis=-1, keepdims=True)
            d = vg - mu
            var = jnp.mean(d * d, axis=-1, keepdims=True)
            vn_buf[rows, lo:lo + SGU_CH] = (d * lax.rsqrt(var + EPS)
                                            * vng_ref[:, lo:lo + SGU_CH]).astype(BF16)
        for g in range(SGU_GROUPS):
            lo = g * SGU_CH
            for c in range(FRONT_SUB // CHUNK):
                t0 = sb * FRONT_SUB + c * CHUNK
                y = _dot(sw_buf[g], vn_buf[t0:t0 + CHUNK, lo:lo + SGU_CH])
                s_buf[t0:t0 + CHUNK, lo:lo + SGU_CH] = y + sgub_ref[:, lo:lo + SGU_CH]
        s_ref[rows, :] = _rms(u * s_buf[rows, :], sgn_ref[...]).astype(BF16)


def _even_front(x, pos, invf, sign, g, wlat, wuv, wkr, qg, wuq, kvg, wukv, vng, sguw, sgub, sgn,
                riders):
    T = x.shape[0]
    tm = FRONT_ROWS
    steps = T // tm
    row = lambda w: pl.BlockSpec((tm, w), lambda i: (i, 0))
    out_widths = (MLA_HEADS * HEAD_SLOT, MLA_HEADS * HEAD_SLOT, MLA_OUT, SGU_OUT)
    r_in, r_out, r_shapes = _cast_rider_specs(riders, steps, lambda i: i)
    consts = (invf, sign, g, wlat, wuv, wkr, qg, wuq, kvg, wukv, vng, sguw, sgub, sgn)
    return pl.pallas_call(
        functools.partial(_even_front_kernel, n_riders=len(riders)),
        grid=(steps,),
        in_specs=[row(D_MODEL), row(1)] + [_const_spec(c.shape) for c in consts] + r_in,
        out_specs=[row(w) for w in out_widths] + r_out,
        out_shape=[jax.ShapeDtypeStruct((T, w), BF16) for w in out_widths] + r_shapes,
        scratch_shapes=[pltpu.VMEM((tm, SGU_OUT), BF16), pltpu.VMEM((tm, SGU_OUT), F32),
                        pltpu.VMEM((SGU_GROUPS, CHUNK, CHUNK), BF16)],
        compiler_params=pltpu.CompilerParams(dimension_semantics=("arbitrary",),
                                             vmem_limit_bytes=VMEM_LIMIT_BYTES),
        name="even_front",
    )(x, pos, *consts, *[w for w, _ in riders])


def _rope_rows():
    lane = jnp.arange(LANES)
    inv_freq = ROPE_BASE ** (-jnp.arange(0, QK_ROPE, 2, dtype=F32) / QK_ROPE)
    invf = inv_freq[lane % HALF_ROPE][None, :]
    sign = jnp.where((lane % QK_ROPE) < HALF_ROPE, -1.0, 1.0).astype(F32)[None, :]
    return invf, sign


def _attn_kernel(*refs, n_riders):
    (q_ref, k_ref, v_ref), rider_in, (o_ref,), rider_out, _ = _split_refs(refs, 3, 1, n_riders)
    _run_cast_riders(rider_in, rider_out)
    seq = q_ref.shape[0]
    tq = ATTN_BLOCK
    row = lax.broadcasted_iota(jnp.int32, (tq, tq), 0)
    col = lax.broadcasted_iota(jnp.int32, (tq, tq), 1)
    causal = col <= row
    for qi in range(seq // tq):
        r0 = qi * tq
        r1 = r0 + tq
        q = q_ref[r0:r1, :]
        sd = jnp.where(causal, _dot_nt(q, k_ref[r0:r1, :]), -jnp.inf)
        m = jnp.max(sd, axis=-1, keepdims=True)
        if qi:
            sp = _dot_nt(q, k_ref[0:r0, :])
            m = jnp.maximum(m, jnp.max(sp, axis=-1, keepdims=True))
        pd = jnp.exp(sd - m)
        l = jnp.sum(pd, axis=-1, keepdims=True)
        acc = _dot(pd.astype(BF16), v_ref[r0:r1, :])
        if qi:
            pp = jnp.exp(sp - m)
            l = l + jnp.sum(pp, axis=-1, keepdims=True)
            acc = acc + _dot(pp.astype(BF16), v_ref[0:r0, :])
        o_ref[r0:r1, :] = (acc / l).astype(BF16)


def _attention(q, k, v, batch, seq, riders):
    T = q.shape[0]
    spec = lambda w: pl.BlockSpec((seq, w), lambda b, h: (b, h))
    r_in, r_out, r_shapes = _cast_rider_specs(riders, batch * MLA_HEADS,
                                              lambda b, h: b * MLA_HEADS + h)
    return pl.pallas_call(
        functools.partial(_attn_kernel, n_riders=len(riders)),
        grid=(batch, MLA_HEADS),
        in_specs=[spec(HEAD_SLOT), spec(HEAD_SLOT), spec(V_HEAD)] + r_in,
        out_specs=[spec(V_HEAD)] + r_out,
        out_shape=[jax.ShapeDtypeStruct((T, MLA_OUT), BF16)] + r_shapes,
        compiler_params=pltpu.CompilerParams(dimension_semantics=("parallel", "parallel"),
                                             vmem_limit_bytes=VMEM_LIMIT_BYTES),
        name="mla_attention",
    )(q, k, v, *[w for w, _ in riders])


def _out_mlp_kernel(x_ref, ma_ref, mb_ref, wo_ref, g_ref, w1_ref, w2_ref, fg_ref, *rest,
                    norm_a, final_norm):
    if norm_a:
        ag_ref, o_ref, h_buf = rest
    else:
        o_ref, h_buf = rest
    k = pl.program_id(1)
    half = ma_ref.shape[1]

    def mlp_chunk(h):
        a = jnp.maximum(_dot(h, w1_ref[...]), 0.0)
        return _dot((a * a).astype(BF16), w2_ref[...])

    @pl.when(k == 0)
    def _():
        for sb in range(x_ref.shape[0] // MLP_SUB):
            rows = slice(sb * MLP_SUB, (sb + 1) * MLP_SUB)
            ma = ma_ref[rows, :]
            if norm_a:
                ma = _rms(ma.astype(F32), ag_ref[...]).astype(BF16)
            x1 = (x_ref[rows, :] + _dot(ma, wo_ref[0:half, :])
                  + _dot(mb_ref[rows, :], wo_ref[half:2 * half, :]))
            h = _rms(x1, g_ref[...]).astype(BF16)
            h_buf[rows, :] = h
            o_ref[rows, :] = x1 + mlp_chunk(h)

    @pl.when(k > 0)
    def _():
        o_ref[...] += mlp_chunk(h_buf[...])

    if final_norm:
        @pl.when(k == pl.num_programs(1) - 1)
        def _():
            o_ref[...] = _rms(o_ref[...], fg_ref[...])


def _out_mlp(x, ma, mb, wo, g, w1, w2, fg, a_gain, *, final_norm):
    T = x.shape[0]
    tm, tf = MLP_ROWS, MLP_FF
    half = wo.shape[0] // 2
    norm_a = a_gain is not None
    in_specs = [pl.BlockSpec((tm, D_MODEL), lambda i, k: (i, 0)),
                pl.BlockSpec((tm, half), lambda i, k: (i, 0)),
                pl.BlockSpec((tm, half), lambda i, k: (i, mb.shape[1] // half - 1)),
                _const_spec(wo.shape), _const_spec(g.shape),
                pl.BlockSpec((D_MODEL, tf), lambda i, k: (0, k)),
                pl.BlockSpec((tf, D_MODEL), lambda i, k: (k, 0)),
                _const_spec(fg.shape)]
    args = [x, ma, mb, wo, g, w1, w2, fg]
    if norm_a:
        in_specs.append(_const_spec(a_gain.shape))
        args.append(a_gain)
    return pl.pallas_call(
        functools.partial(_out_mlp_kernel, norm_a=norm_a, final_norm=final_norm),
        grid=(T // tm, D_FF // tf),
        in_specs=in_specs,
        out_specs=pl.BlockSpec((tm, D_MODEL), lambda i, k: (i, 0)),
        out_shape=jax.ShapeDtypeStruct((T, D_MODEL), F32),
        scratch_shapes=[pltpu.VMEM((tm, D_MODEL), BF16)],
        compiler_params=pltpu.CompilerParams(dimension_semantics=("parallel", "arbitrary"),
                                             vmem_limit_bytes=VMEM_LIMIT_BYTES),
        name="out_proj_mlp",
    )(*args)


def _conv_front_kernel(*refs, n_riders, tiles_per_seq):
    ins, rider_in, (o_ref,), rider_out, (hist,) = _split_refs(refs, 4, 1, n_riders)
    x_ref, g_ref, w_ref, cw_ref = ins
    _run_cast_riders(rider_in, rider_out)
    tm = x_ref.shape[0]
    tn = CONV_COLS
    i = pl.program_id(0)
    h = _rms(x_ref[...], g_ref[...]).astype(BF16)

    @pl.when(i % tiles_per_seq == 0)
    def _():
        hist[...] = jnp.zeros_like(hist)

    row = lax.broadcasted_iota(jnp.int32, (tm, tn), 0)
    for cb in range(CONV_DIM // tn):
        c0 = cb * tn
        z = (_dot(h, w_ref[:, CONV_DIM + c0:CONV_DIM + c0 + tn])
             * _dot(h, w_ref[:, 2 * CONV_DIM + c0:2 * CONV_DIM + c0 + tn]))
        p1 = hist[7:8, c0:c0 + tn]
        p2 = hist[6:7, c0:c0 + tn]
        z1 = jnp.where(row == 0, p1, pltpu.roll(z, 1, 0))
        z2 = jnp.where(row == 0, p2, jnp.where(row == 1, p1, pltpu.roll(z, 2, 0)))
        conv = (cw_ref[0:1, c0:c0 + tn] * z2 + cw_ref[1:2, c0:c0 + tn] * z1
                + cw_ref[2:3, c0:c0 + tn] * z)
        o_ref[:, c0:c0 + tn] = (_dot(h, w_ref[:, c0:c0 + tn]) * conv).astype(BF16)
        hist[:, c0:c0 + tn] = z[tm - 8:tm, :]


def _conv_front(x, g, w_in, conv_w, layer, seq, riders):
    T = x.shape[0]
    tm = CONV_ROWS
    steps = T // tm
    r_in, r_out, r_shapes = _cast_rider_specs(riders, steps, lambda i: i)
    return pl.pallas_call(
        functools.partial(_conv_front_kernel, n_riders=len(riders), tiles_per_seq=seq // tm),
        grid=(steps,),
        in_specs=[pl.BlockSpec((tm, D_MODEL), lambda i: (i, 0)), _const_spec(g.shape),
                  _const_spec(w_in.shape), _layer_spec(conv_w, layer)] + r_in,
        out_specs=[pl.BlockSpec((tm, CONV_DIM), lambda i: (i, 0))] + r_out,
        out_shape=[jax.ShapeDtypeStruct((T, CONV_DIM), BF16)] + r_shapes,
        scratch_shapes=[pltpu.VMEM((8, CONV_DIM), F32)],
        compiler_params=pltpu.CompilerParams(dimension_semantics=("arbitrary",),
                                             vmem_limit_bytes=VMEM_LIMIT_BYTES),
        name="conv_front",
    )(x, g, w_in, conv_w, *[w for w, _ in riders])


def _even_weights(w_in, w_uq, w_ukv):
    c1 = Q_LORA
    c2 = c1 + KV_LORA
    c3 = c2 + QK_ROPE
    k1 = w_in[:, c2:c2 + HALF_ROPE]
    k2 = w_in[:, c2 + HALF_ROPE:c3]
    wlat = w_in[:, :c2]
    wuv = w_in[:, c3:]
    wkr = jnp.concatenate([k1, k2, k1, k2, k2, k1, k2, k1], axis=1)
    wq = w_uq.reshape(Q_LORA, MLA_HEADS, QK_NOPE + QK_ROPE)
    r1 = wq[:, :, QK_NOPE:QK_NOPE + HALF_ROPE]
    r2 = wq[:, :, QK_NOPE + HALF_ROPE:]
    pairs = LANES // QK_ROPE

    def rope_cols(a, b):
        w = jnp.concatenate([a, b], axis=-1).reshape(Q_LORA, pairs, MLA_HEADS // pairs, QK_ROPE)
        return w.transpose(0, 2, 1, 3).reshape(Q_LORA, -1)

    wuq = jnp.concatenate([wq[:, :, :QK_NOPE].reshape(Q_LORA, -1), rope_cols(r1, r2),
                           rope_cols(r2, r1)], axis=1)
    wkv = w_ukv.reshape(KV_LORA, MLA_HEADS, QK_NOPE + V_HEAD)
    wukv = jnp.concatenate([wkv[:, :, :QK_NOPE].reshape(KV_LORA, -1),
                            wkv[:, :, QK_NOPE:].reshape(KV_LORA, -1)], axis=1)
    return tuple(w.astype(BF16) for w in (wlat, wuv, wkr, wuq, wukv))


def kernel(x, positions, e_norm_mix, e_w_in, e_q_norm, e_w_uq, e_kv_norm, e_w_ukv, e_v_norm,
           e_sgu_w, e_sgu_b, e_mla_out_norm, e_sgu_out_norm, e_w_out, o_norm_mix, o_w_in,
           o_conv_w, o_w_out, mlp_norm, mlp_w1, mlp_w2, final_norm):
    batch, seq, d = x.shape
    T = batch * seq
    depth = mlp_norm.shape[0]
    xf = x.reshape(T, d)
    pos = positions.reshape(T, 1)
    invf, sign = _rope_rows()
    fg = final_norm[None, :]
    row2d = lambda a: a.reshape(1, -1)

    o_wi = None
    for layer in range(depth):
        i = layer // 2
        mlp_riders = [(mlp_w1, layer), (mlp_w2, layer)]
        if layer % 2 == 0:
            wlat, wuv, wkr, wuq, wukv = _even_weights(e_w_in[i], e_w_uq[i], e_w_ukv[i])
            sgub = jnp.repeat(e_sgu_b[i].T, SGU_CH, axis=1)
            q, k, v, sn, wo = _even_front(
                xf, pos, invf, sign, row2d(e_norm_mix[i]), wlat, wuv, wkr, row2d(e_q_norm[i]), wuq,
                row2d(e_kv_norm[i]), wukv, row2d(e_v_norm[i]), e_sgu_w[i], sgub,
                row2d(e_sgu_out_norm[i]), riders=[(e_w_out, i)])
            next_in = [(o_w_in, (layer + 1) // 2)] if layer + 1 < depth else []
            ma, w1, w2, *rest = _attention(q, k, v, batch, seq, mlp_riders + next_in)
            o_wi = rest[0] if rest else None
            mb, a_gain = sn, row2d(e_mla_out_norm[i])
        else:
            ma, wo, w1, w2 = _conv_front(xf, row2d(o_norm_mix[i]), o_wi, o_conv_w, i, seq,
                                         riders=[(o_w_out, i)] + mlp_riders)
            mb, a_gain = ma, None
        xf = _out_mlp(xf, ma, mb, wo, row2d(mlp_norm[layer]), w1, w2, fg, a_gain,
                      final_norm=(layer == depth - 1))
    return xf.reshape(batch, seq, d)
```

```python
import functools
import math

import jax
import jax.numpy as jnp
from jax import lax
from jax.experimental import pallas as pl
from jax.experimental.pallas import tpu as pltpu

D_MODEL = 2048
MLA_HEADS = 8
Q_LORA = 512
KV_LORA = 512
QK_NOPE = 128
QK_ROPE = 64
V_HEAD = 128
ROPE_BASE = 10000.0
SGU_GROUPS = 8
SGU_CH = 128
CHUNK = 128
CONV_DIM = D_MODEL
CONV_WIDTH = 3
D_FF = 4 * D_MODEL
EPS = 1e-6
MLA_OUT = MLA_HEADS * V_HEAD
SGU_OUT = SGU_GROUPS * SGU_CH
HALF_ROPE = QK_ROPE // 2
LOG2_E = math.log2(math.e)
HEAD_SLOT = 2 * QK_NOPE

LANES = 128
BF16_ROWS = 16
VMEM_LIMIT_BYTES = 60 * 1024 * 1024

FRONT_ROWS = 512
FRONT_SUB = 256
ATTN_BLOCK = 256
ATTN_AHEAD = 2
MLP_ROWS = 512
MLP_SUB = 256
MLP_FF = 1024
CONV_ROWS = 512
CONV_COLS = 512

BF16 = jnp.bfloat16
F32 = jnp.float32


def _dot(a, b):
    return jnp.dot(a, b, preferred_element_type=F32)


def _dot_nt(a, b):
    return lax.dot_general(a, b, (((1,), (1,)), ((), ())), preferred_element_type=F32)


def _rms(x, g):
    return x * lax.rsqrt(jnp.mean(x * x, axis=-1, keepdims=True) + EPS) * g


def _const_spec(shape):
    return pl.BlockSpec(shape, lambda *_: (0,) * len(shape), pipeline_mode=pl.Buffered(1))


def _layer_spec(w, layer):
    return pl.BlockSpec((None,) + w.shape[1:], lambda *_: (layer, 0, 0), pipeline_mode=pl.Buffered(1))


def _cast_rider_specs(riders, steps, step_of):
    ins, outs, shapes = [], [], []
    for w, layer in riders:
        _, rows, cols = w.shape
        rb = rows // steps
        assert rb * steps == rows and rb % BF16_ROWS == 0, (w.shape, steps)
        ins.append(pl.BlockSpec((None, rb, cols), lambda *g, layer=layer: (layer, step_of(*g), 0)))
        outs.append(pl.BlockSpec((rb, cols), lambda *g: (step_of(*g), 0)))
        shapes.append(jax.ShapeDtypeStruct((rows, cols), BF16))
    return ins, outs, shapes


def _split_refs(refs, n_in, n_out, n_riders):
    cuts = [n_in, n_in + n_riders, n_in + n_riders + n_out, n_in + 2 * n_riders + n_out]
    return (refs[:cuts[0]], refs[cuts[0]:cuts[1]], refs[cuts[1]:cuts[2]], refs[cuts[2]:cuts[3]],
            refs[cuts[3]:])


def _run_cast_riders(srcs, dsts):
    for src, dst in zip(srcs, dsts):
        dst[...] = src[...].astype(BF16)


def _rope_tables(pos_ref, r0, invf, lane_group):
    n = LANES // HALF_ROPE
    seg = lane_group.shape[0]
    packed = jnp.zeros((seg, LANES), F32)
    for j in range(n):
        p = pos_ref[r0 + j * seg:r0 + (j + 1) * seg, :].astype(F32)
        packed = jnp.where(lane_group == j, p, packed)
    ang = packed * invf
    tables = []
    for t in (jnp.cos(ang), jnp.sin(ang)):
        parts = []
        for j in range(n):
            one = jnp.where(lane_group == j, t, 0.0)
            full = one
            for s in range(1, n):
                full = full + pltpu.roll(one, s * HALF_ROPE, 1)
            parts.append(full)
        tables.append(jnp.concatenate(parts, axis=0))
    return tables


def _even_front_kernel(*refs, n_riders):
    ins, rider_in, outs, rider_out, (vn_buf, s_buf, sw_buf) = _split_refs(refs, 16, 4, n_riders)
    (x_ref, pos_ref, invf_ref, sign_ref, g_ref, wlat_ref, wuv_ref, wkr_ref, qg_ref, wuq_ref,
     kvg_ref, wukv_ref, vng_ref, sguw_ref, sgub_ref, sgn_ref) = ins
    q_ref, k_ref, v_ref, s_ref = outs

    @pl.when(pl.program_id(0) == 0)
    def _():
        row = lax.broadcasted_iota(jnp.int32, (CHUNK, CHUNK), 0)
        col = lax.broadcasted_iota(jnp.int32, (CHUNK, CHUNK), 1)
        for g in range(SGU_GROUPS):
            sw_buf[g] = jnp.where(col <= row, sguw_ref[g], 0.0).astype(BF16)

    scale = (QK_NOPE + QK_ROPE) ** -0.5 * LOG2_E
    n_rope_cols = MLA_HEADS * QK_ROPE // LANES
    low_lanes = lax.broadcasted_iota(jnp.int32, (FRONT_SUB, LANES), 1) < QK_ROPE
    seg = FRONT_SUB // (LANES // HALF_ROPE)
    lane_group = lax.broadcasted_iota(jnp.int32, (seg, LANES), 1) // HALF_ROPE

    def normed(sb):
        return _rms(x_ref[sb * FRONT_SUB:(sb + 1) * FRONT_SUB, :], g_ref[...]).astype(BF16)

    def mla_part(sb, h):
        rows = slice(sb * FRONT_SUB, (sb + 1) * FRONT_SUB)
        cos, sin = _rope_tables(pos_ref, sb * FRONT_SUB, invf_ref[...], lane_group)
        sin = sin * sign_ref[...]

        cq = _rms(_dot(h, wlat_ref[:, 0:Q_LORA]), qg_ref[...]).astype(BF16)
        q = _dot(cq, wuq_ref[...])
        cos_q = jnp.concatenate([cos] * n_rope_cols, axis=1)
        sin_q = jnp.concatenate([sin] * n_rope_cols, axis=1)
        r0 = MLA_OUT
        r1 = r0 + n_rope_cols * LANES
        qn = (q[:, :MLA_OUT] * scale).astype(BF16)
        qr = ((q[:, r0:r1] * cos_q + q[:, r1:] * sin_q) * scale).astype(BF16)

        ckv = _rms(_dot(h, wlat_ref[:, Q_LORA:Q_LORA + KV_LORA]), kvg_ref[...]).astype(BF16)
        kv = _dot(ckv, wukv_ref[...])
        kn = kv[:, :MLA_OUT].astype(BF16)
        v_ref[rows, :] = kv[:, MLA_OUT:].astype(BF16)

        ka = _dot(h, wkr_ref[:, 0:LANES])
        kb = _dot(h, wkr_ref[:, LANES:2 * LANES])
        kr = (ka * cos + kb * sin).astype(BF16)
        for hd in range(MLA_HEADS):
            lo = hd * HEAD_SLOT
            src = slice(hd * QK_NOPE, (hd + 1) * QK_NOPE)
            pair = hd % n_rope_cols
            rope = qr[:, pair * LANES:(pair + 1) * LANES]
            keep = low_lanes if hd < n_rope_cols else ~low_lanes
            q_ref[rows, lo:lo + QK_NOPE] = qn[:, src]
            q_ref[rows, lo + QK_NOPE:lo + HEAD_SLOT] = jnp.where(keep, rope, jnp.zeros_like(rope))
            k_ref[rows, lo:lo + QK_NOPE] = kn[:, src]
            k_ref[rows, lo + QK_NOPE:lo + HEAD_SLOT] = kr

    def sgu_part(sb, h):
        rows = slice(sb * FRONT_SUB, (sb + 1) * FRONT_SUB)
        u = jax.nn.gelu(_dot(h, wuv_ref[:, 0:SGU_OUT]))
        vv = jax.nn.gelu(_dot(h, wuv_ref[:, SGU_OUT:2 * SGU_OUT]))
        for g in range(SGU_GROUPS):
            lo = g * SGU_CH
            vg = vv[:, lo:lo + SGU_CH]
            mu = jnp.mean(vg, axis=-1, keepdims=True)
            d = vg - mu
            var = jnp.mean(d * d, axis=-1, keepdims=True)
            vn_buf[rows, lo:lo + SGU_CH] = (d * lax.rsqrt(var + EPS)
                                            * vng_ref[:, lo:lo + SGU_CH]).astype(BF16)
        for g in range(SGU_GROUPS):
            lo = g * SGU_CH
            for c in range(FRONT_SUB // CHUNK):
                t0 = sb * FRONT_SUB + c * CHUNK
                y = _dot(sw_buf[g], vn_buf[t0:t0 + CHUNK, lo:lo + SGU_CH])
                s_buf[t0:t0 + CHUNK, lo:lo + SGU_CH] = y + sgub_ref[:, lo:lo + SGU_CH]
        s_ref[rows, :] = _rms(u * s_buf[rows, :], sgn_ref[...]).astype(BF16)

    subs = range(x_ref.shape[0] // FRONT_SUB)
    hs = [normed(sb) for sb in subs]
    for sb in subs:
        sgu_part(sb, hs[sb])
    for sb in subs:
        mla_part(sb, hs[sb])
    _run_cast_riders(rider_in, rider_out)


def _even_front(x, pos, invf, sign, g, wlat, wuv, wkr, qg, wuq, kvg, wukv, vng, sguw, sgub, sgn,
                riders):
    T = x.shape[0]
    tm = FRONT_ROWS
    steps = T // tm
    row = lambda w: pl.BlockSpec((tm, w), lambda i: (i, 0))
    out_widths = (MLA_HEADS * HEAD_SLOT, MLA_HEADS * HEAD_SLOT, MLA_OUT, SGU_OUT)
    r_in, r_out, r_shapes = _cast_rider_specs(riders, steps, lambda i: i)
    consts = (invf, sign, g, wlat, wuv, wkr, qg, wuq, kvg, wukv, vng, sguw, sgub, sgn)
    return pl.pallas_call(
        functools.partial(_even_front_kernel, n_riders=len(riders)),
        grid=(steps,),
        in_specs=[row(D_MODEL), row(1)] + [_const_spec(c.shape) for c in consts] + r_in,
        out_specs=[row(w) for w in out_widths] + r_out,
        out_shape=[jax.ShapeDtypeStruct((T, w), BF16) for w in out_widths] + r_shapes,
        scratch_shapes=[pltpu.VMEM((tm, SGU_OUT), BF16), pltpu.VMEM((tm, SGU_OUT), F32),
                        pltpu.VMEM((SGU_GROUPS, CHUNK, CHUNK), BF16)],
        compiler_params=pltpu.CompilerParams(dimension_semantics=("arbitrary",),
                                             vmem_limit_bytes=VMEM_LIMIT_BYTES),
        name="even_front",
    )(x, pos, *consts, *[w for w, _ in riders])


def _rope_rows():
    lane = jnp.arange(LANES)
    inv_freq = ROPE_BASE ** (-jnp.arange(0, QK_ROPE, 2, dtype=F32) / QK_ROPE)
    invf = inv_freq[lane % HALF_ROPE][None, :]
    sign = jnp.where((lane % QK_ROPE) < HALF_ROPE, -1.0, 1.0).astype(F32)[None, :]
    return invf, sign


def _attn_kernel(*refs, n_riders):
    (q_ref, k_ref, v_ref), rider_in, (o_ref,), rider_out, (v_ext,) = _split_refs(refs, 3, 1, n_riders)
    _run_cast_riders(rider_in, rider_out)
    seq = q_ref.shape[0]
    tq = ATTN_BLOCK
    row = lax.broadcasted_iota(jnp.int32, (tq, tq), 0)
    col = lax.broadcasted_iota(jnp.int32, (tq, tq), 1)
    causal = col <= row
    v_ext[:, 0:V_HEAD] = v_ref[...]
    v_ext[:, V_HEAD:2 * V_HEAD] = jnp.ones((seq, V_HEAD), BF16)

    def scores(qi):
        r0 = qi * tq
        q = q_ref[r0:r0 + tq, :]
        sd = jnp.where(causal, _dot_nt(q, k_ref[r0:r0 + tq, :]), -jnp.inf)
        sp = _dot_nt(q, k_ref[0:r0, :]) if qi else None
        return sd, sp

    order = list(reversed(range(seq // tq)))
    ahead = [scores(qi) for qi in order[:ATTN_AHEAD]]
    for n, qi in enumerate(order):
        r0 = qi * tq
        r1 = r0 + tq
        sd, sp = ahead.pop(0)
        if n + ATTN_AHEAD < len(order):
            ahead.append(scores(order[n + ATTN_AHEAD]))
        m = jnp.max(sd, axis=-1, keepdims=True)
        if qi:
            m = jnp.maximum(m, jnp.max(sp, axis=-1, keepdims=True))
        acc = _dot(jnp.exp2(sd - m).astype(BF16), v_ext[r0:r1, :])
        if qi:
            acc = acc + _dot(jnp.exp2(sp - m).astype(BF16), v_ext[0:r0, :])
        o_ref[r0:r1, :] = (acc[:, 0:V_HEAD] / acc[:, V_HEAD:2 * V_HEAD]).astype(BF16)


def _attention(q, k, v, batch, seq, riders):
    T = q.shape[0]
    spec = lambda w: pl.BlockSpec((seq, w), lambda b, h: (b, h))
    r_in, r_out, r_shapes = _cast_rider_specs(riders, batch * MLA_HEADS,
                                              lambda b, h: b * MLA_HEADS + h)
    return pl.pallas_call(
        functools.partial(_attn_kernel, n_riders=len(riders)),
        grid=(batch, MLA_HEADS),
        in_specs=[spec(HEAD_SLOT), spec(HEAD_SLOT), spec(V_HEAD)] + r_in,
        out_specs=[spec(V_HEAD)] + r_out,
        out_shape=[jax.ShapeDtypeStruct((T, MLA_OUT), BF16)] + r_shapes,
        scratch_shapes=[pltpu.VMEM((seq, 2 * V_HEAD), BF16)],
        compiler_params=pltpu.CompilerParams(dimension_semantics=("parallel", "parallel"),
                                             vmem_limit_bytes=VMEM_LIMIT_BYTES),
        name="mla_attention",
    )(q, k, v, *[w for w, _ in riders])


def _out_mlp_kernel(x_ref, ma_ref, mb_ref, wo_ref, g_ref, w1_ref, w2_ref, fg_ref, *rest,
                    norm_a, final_norm):
    if norm_a:
        ag_ref, o_ref, h_buf = rest
    else:
        o_ref, h_buf = rest
    k = pl.program_id(1)
    half = ma_ref.shape[1]

    def mlp_chunk(h):
        a = jnp.maximum(_dot(h, w1_ref[...]), 0.0)
        return _dot((a * a).astype(BF16), w2_ref[...])

    @pl.when(k == 0)
    def _():
        for sb in range(x_ref.shape[0] // MLP_SUB):
            rows = slice(sb * MLP_SUB, (sb + 1) * MLP_SUB)
            ma = ma_ref[rows, :]
            if norm_a:
                ma = _rms(ma.astype(F32), ag_ref[...]).astype(BF16)
            x1 = (x_ref[rows, :] + _dot(ma, wo_ref[0:half, :])
                  + _dot(mb_ref[rows, :], wo_ref[half:2 * half, :]))
            h = _rms(x1, g_ref[...]).astype(BF16)
            h_buf[rows, :] = h
            o_ref[rows, :] = x1 + mlp_chunk(h)

    @pl.when(k > 0)
    def _():
        o_ref[...] += mlp_chunk(h_buf[...])

    if final_norm:
        @pl.when(k == pl.num_programs(1) - 1)
        def _():
            o_ref[...] = _rms(o_ref[...], fg_ref[...])


def _out_mlp(x, ma, mb, wo, g, w1, w2, fg, a_gain, *, final_norm):
    T = x.shape[0]
    tm, tf = MLP_ROWS, MLP_FF
    half = wo.shape[0] // 2
    norm_a = a_gain is not None
    in_specs = [pl.BlockSpec((tm, D_MODEL), lambda i, k: (i, 0)),
                pl.BlockSpec((tm, half), lambda i, k: (i, 0)),
                pl.BlockSpec((tm, half), lambda i, k: (i, mb.shape[1] // half - 1)),
                _const_spec(wo.shape), _const_spec(g.shape),
                pl.BlockSpec((D_MODEL, tf), lambda i, k: (0, k)),
                pl.BlockSpec((tf, D_MODEL), lambda i, k: (k, 0)),
                _const_spec(fg.shape)]
    args = [x, ma, mb, wo, g, w1, w2, fg]
    if norm_a:
        in_specs.append(_const_spec(a_gain.shape))
        args.append(a_gain)
    return pl.pallas_call(
        functools.partial(_out_mlp_kernel, norm_a=norm_a, final_norm=final_norm),
        grid=(T // tm, D_FF // tf),
        in_specs=in_specs,
        out_specs=pl.BlockSpec((tm, D_MODEL), lambda i, k: (i, 0)),
        out_shape=jax.ShapeDtypeStruct((T, D_MODEL), F32),
        scratch_shapes=[pltpu.VMEM((tm, D_MODEL), BF16)],
        compiler_params=pltpu.CompilerParams(dimension_semantics=("parallel", "arbitrary"),
                                             vmem_limit_bytes=VMEM_LIMIT_BYTES),
        name="out_proj_mlp",
    )(*args)


def _conv_front_kernel(*refs, n_riders, tiles_per_seq):
    ins, rider_in, (o_ref,), rider_out, (hist,) = _split_refs(refs, 4, 1, n_riders)
    x_ref, g_ref, w_ref, cw_ref = ins
    tm = x_ref.shape[0]
    tn = CONV_COLS

    @pl.when(pl.program_id(0) == 0)
    def _():
        hist[...] = jnp.zeros_like(hist)

    seq_start = pl.program_id(0) % tiles_per_seq == 0
    h = _rms(x_ref[...], g_ref[...]).astype(BF16)
    row = lax.broadcasted_iota(jnp.int32, (tm, tn), 0)
    for cb in range(CONV_DIM // tn):
        c0 = cb * tn
        z = (_dot(h, w_ref[:, CONV_DIM + c0:CONV_DIM + c0 + tn])
             * _dot(h, w_ref[:, 2 * CONV_DIM + c0:2 * CONV_DIM + c0 + tn]))
        p1 = jnp.where(seq_start, 0.0, hist[7:8, c0:c0 + tn])
        p2 = jnp.where(seq_start, 0.0, hist[6:7, c0:c0 + tn])
        z1 = jnp.where(row == 0, p1, pltpu.roll(z, 1, 0))
        z2 = jnp.where(row == 0, p2, jnp.where(row == 1, p1, pltpu.roll(z, 2, 0)))
        conv = (cw_ref[0:1, c0:c0 + tn] * z2 + cw_ref[1:2, c0:c0 + tn] * z1
                + cw_ref[2:3, c0:c0 + tn] * z)
        o_ref[:, c0:c0 + tn] = (_dot(h, w_ref[:, c0:c0 + tn]) * conv).astype(BF16)
        hist[:, c0:c0 + tn] = z[tm - 8:tm, :]
    _run_cast_riders(rider_in, rider_out)


def _conv_front(x, g, w_in, conv_w, layer, seq, riders):
    T = x.shape[0]
    tm = CONV_ROWS
    steps = T // tm
    r_in, r_out, r_shapes = _cast_rider_specs(riders, steps, lambda i: i)
    return pl.pallas_call(
        functools.partial(_conv_front_kernel, n_riders=len(riders), tiles_per_seq=seq // tm),
        grid=(steps,),
        in_specs=[pl.BlockSpec((tm, D_MODEL), lambda i: (i, 0)), _const_spec(g.shape),
                  _const_spec(w_in.shape), _layer_spec(conv_w, layer)] + r_in,
        out_specs=[pl.BlockSpec((tm, CONV_DIM), lambda i: (i, 0))] + r_out,
        out_shape=[jax.ShapeDtypeStruct((T, CONV_DIM), BF16)] + r_shapes,
        scratch_shapes=[pltpu.VMEM((8, CONV_DIM), F32)],
        compiler_params=pltpu.CompilerParams(dimension_semantics=("arbitrary",),
                                             vmem_limit_bytes=VMEM_LIMIT_BYTES),
        name="conv_front",
    )(x, g, w_in, conv_w, *[w for w, _ in riders])


def _even_weights(w_in, w_uq, w_ukv):
    c1 = Q_LORA
    c2 = c1 + KV_LORA
    c3 = c2 + QK_ROPE
    k1 = w_in[:, c2:c2 + HALF_ROPE]
    k2 = w_in[:, c2 + HALF_ROPE:c3]
    wlat = w_in[:, :c2]
    wuv = w_in[:, c3:]
    wkr = jnp.concatenate([k1, k2, k1, k2, k2, k1, k2, k1], axis=1)
    wq = w_uq.reshape(Q_LORA, MLA_HEADS, QK_NOPE + QK_ROPE)
    r1 = wq[:, :, QK_NOPE:QK_NOPE + HALF_ROPE]
    r2 = wq[:, :, QK_NOPE + HALF_ROPE:]
    pairs = LANES // QK_ROPE

    def rope_cols(a, b):
        w = jnp.concatenate([a, b], axis=-1).reshape(Q_LORA, pairs, MLA_HEADS // pairs, QK_ROPE)
        return w.transpose(0, 2, 1, 3).reshape(Q_LORA, -1)

    wuq = jnp.concatenate([wq[:, :, :QK_NOPE].reshape(Q_LORA, -1), rope_cols(r1, r2),
                           rope_cols(r2, r1)], axis=1)
    wkv = w_ukv.reshape(KV_LORA, MLA_HEADS, QK_NOPE + V_HEAD)
    wukv = jnp.concatenate([wkv[:, :, :QK_NOPE].reshape(KV_LORA, -1),
                            wkv[:, :, QK_NOPE:].reshape(KV_LORA, -1)], axis=1)
    return tuple(w.astype(BF16) for w in (wlat, wuv, wkr, wuq, wukv))


def kernel(x, positions, e_norm_mix, e_w_in, e_q_norm, e_w_uq, e_kv_norm, e_w_ukv, e_v_norm,
           e_sgu_w, e_sgu_b, e_mla_out_norm, e_sgu_out_norm, e_w_out, o_norm_mix, o_w_in,
           o_conv_w, o_w_out, mlp_norm, mlp_w1, mlp_w2, final_norm):
    batch, seq, d = x.shape
    T = batch * seq
    depth = mlp_norm.shape[0]
    xf = x.reshape(T, d)
    pos = positions.reshape(T, 1)
    invf, sign = _rope_rows()
    fg = final_norm[None, :]
    row2d = lambda a: a.reshape(1, -1)

    o_wi = None
    for layer in range(depth):
        i = layer // 2
        mlp_riders = [(mlp_w1, layer), (mlp_w2, layer)]
        if layer % 2 == 0:
            wlat, wuv, wkr, wuq, wukv = _even_weights(e_w_in[i], e_w_uq[i], e_w_ukv[i])
            sgub = jnp.repeat(e_sgu_b[i].T, SGU_CH, axis=1)
            q, k, v, sn, wo = _even_front(
                xf, pos, invf, sign, row2d(e_norm_mix[i]), wlat, wuv, wkr, row2d(e_q_norm[i]), wuq,
                row2d(e_kv_norm[i]), wukv, row2d(e_v_norm[i]), e_sgu_w[i], sgub,
                row2d(e_sgu_out_norm[i]), riders=[(e_w_out, i)])
            next_in = [(o_w_in, (layer + 1) // 2)] if layer + 1 < depth else []
            ma, w1, w2, *rest = _attention(q, k, v, batch, seq, mlp_riders + next_in)
            o_wi = rest[0] if rest else None
            mb, a_gain = sn, row2d(e_mla_out_norm[i])
        else:
            ma, wo, w1, w2 = _conv_front(xf, row2d(o_norm_mix[i]), o_wi, o_conv_w, i, seq,
                                         riders=[(o_w_out, i)] + mlp_riders)
            mb, a_gain = ma, None
        xf = _out_mlp(xf, ma, mb, wo, row2d(mlp_norm[layer]), w1, w2, fg, a_gain,
                      final_norm=(layer == depth - 1))
    return xf.reshape(batch, seq, d)
```

```python
import functools
import math

import jax
import jax.numpy as jnp
from jax import lax
from jax.experimental import pallas as pl
from jax.experimental.pallas import tpu as pltpu

D_MODEL = 2048
MLA_HEADS = 8
Q_LORA = 512
KV_LORA = 512
QK_NOPE = 128
QK_ROPE = 64
V_HEAD = 128
ROPE_BASE = 10000.0
SGU_GROUPS = 8
SGU_CH = 128
CHUNK = 128
CONV_DIM = D_MODEL
CONV_WIDTH = 3
D_FF = 4 * D_MODEL
EPS = 1e-6
MLA_OUT = MLA_HEADS * V_HEAD
SGU_OUT = SGU_GROUPS * SGU_CH
HALF_ROPE = QK_ROPE // 2
LOG2_E = math.log2(math.e)
HEAD_SLOT = 2 * QK_NOPE

LANES = 128
BF16_ROWS = 16
VMEM_LIMIT_BYTES = 60 * 1024 * 1024

FRONT_ROWS = 512
FRONT_SUB = 256
ATTN_BLOCK = 256
ATTN_AHEAD = 2
MLP_ROWS = 512
MLP_SUB = 256
MLP_FF = 1024
MLP_SLOTS = 2
CONV_ROWS = 512
CONV_COLS = 512

BF16 = jnp.bfloat16
F32 = jnp.float32


def _dot(a, b):
    return jnp.dot(a, b, preferred_element_type=F32)


def _dot_nt(a, b):
    return lax.dot_general(a, b, (((1,), (1,)), ((), ())), preferred_element_type=F32)


def _rms(x, g):
    return x * lax.rsqrt(jnp.mean(x * x, axis=-1, keepdims=True) + EPS) * g


def _const_spec(shape):
    return pl.BlockSpec(shape, lambda *_: (0,) * len(shape), pipeline_mode=pl.Buffered(1))


def _layer_spec(w, layer):
    return pl.BlockSpec((None,) + w.shape[1:], lambda *_: (layer, 0, 0), pipeline_mode=pl.Buffered(1))


def _cast_rider_specs(riders, steps, step_of):
    ins, outs, shapes = [], [], []
    for w, layer in riders:
        _, rows, cols = w.shape
        rb = rows // steps
        assert rb * steps == rows and rb % BF16_ROWS == 0, (w.shape, steps)
        ins.append(pl.BlockSpec((None, rb, cols), lambda *g, layer=layer: (layer, step_of(*g), 0)))
        outs.append(pl.BlockSpec((rb, cols), lambda *g: (step_of(*g), 0)))
        shapes.append(jax.ShapeDtypeStruct((rows, cols), BF16))
    return ins, outs, shapes


def _split_refs(refs, n_in, n_out, n_riders):
    cuts = [n_in, n_in + n_riders, n_in + n_riders + n_out, n_in + 2 * n_riders + n_out]
    return (refs[:cuts[0]], refs[cuts[0]:cuts[1]], refs[cuts[1]:cuts[2]], refs[cuts[2]:cuts[3]],
            refs[cuts[3]:])


def _run_cast_riders(srcs, dsts):
    for src, dst in zip(srcs, dsts):
        dst[...] = src[...].astype(BF16)


def _rope_tables(pos_ref, r0, invf, lane_group):
    n = LANES // HALF_ROPE
    seg = lane_group.shape[0]
    packed = jnp.zeros((seg, LANES), F32)
    for j in range(n):
        p = pos_ref[r0 + j * seg:r0 + (j + 1) * seg, :].astype(F32)
        packed = jnp.where(lane_group == j, p, packed)
    ang = packed * invf
    tables = []
    for t in (jnp.cos(ang), jnp.sin(ang)):
        parts = []
        for j in range(n):
            one = jnp.where(lane_group == j, t, 0.0)
            full = one
            for s in range(1, n):
                full = full + pltpu.roll(one, s * HALF_ROPE, 1)
            parts.append(full)
        tables.append(jnp.concatenate(parts, axis=0))
    return tables


def _even_front_kernel(*refs, n_riders):
    ins, rider_in, outs, rider_out, (vn_buf, s_buf, sw_buf) = _split_refs(refs, 16, 4, n_riders)
    (x_ref, pos_ref, invf_ref, sign_ref, g_ref, wlat_ref, wuv_ref, wkr_ref, qg_ref, wuq_ref,
     kvg_ref, wukv_ref, vng_ref, sguw_ref, sgub_ref, sgn_ref) = ins
    q_ref, k_ref, v_ref, s_ref = outs

    @pl.when(pl.program_id(0) == 0)
    def _():
        row = lax.broadcasted_iota(jnp.int32, (CHUNK, CHUNK), 0)
        col = lax.broadcasted_iota(jnp.int32, (CHUNK, CHUNK), 1)
        for g in range(SGU_GROUPS):
            sw_buf[g] = jnp.where(col <= row, sguw_ref[g], 0.0).astype(BF16)

    scale = (QK_NOPE + QK_ROPE) ** -0.5 * LOG2_E
    n_rope_cols = MLA_HEADS * QK_ROPE // LANES
    low_lanes = lax.broadcasted_iota(jnp.int32, (FRONT_SUB, LANES), 1) < QK_ROPE
    seg = FRONT_SUB // (LANES // HALF_ROPE)
    lane_group = lax.broadcasted_iota(jnp.int32, (seg, LANES), 1) // HALF_ROPE

    def normed(sb):
        return _rms(x_ref[sb * FRONT_SUB:(sb + 1) * FRONT_SUB, :], g_ref[...]).astype(BF16)

    def mla_part(sb, h):
        rows = slice(sb * FRONT_SUB, (sb + 1) * FRONT_SUB)
        cos, sin = _rope_tables(pos_ref, sb * FRONT_SUB, invf_ref[...], lane_group)
        sin = sin * sign_ref[...]

        cq = _rms(_dot(h, wlat_ref[:, 0:Q_LORA]), qg_ref[...]).astype(BF16)
        q = _dot(cq, wuq_ref[...])
        cos_q = jnp.concatenate([cos] * n_rope_cols, axis=1)
        sin_q = jnp.concatenate([sin] * n_rope_cols, axis=1)
        r0 = MLA_OUT
        r1 = r0 + n_rope_cols * LANES
        qn = (q[:, :MLA_OUT] * scale).astype(BF16)
        qr = ((q[:, r0:r1] * cos_q + q[:, r1:] * sin_q) * scale).astype(BF16)

        ckv = _rms(_dot(h, wlat_ref[:, Q_LORA:Q_LORA + KV_LORA]), kvg_ref[...]).astype(BF16)
        kv = _dot(ckv, wukv_ref[...])
        kn = kv[:, :MLA_OUT].astype(BF16)
        v_ref[rows, :] = kv[:, MLA_OUT:].astype(BF16)

        ka = _dot(h, wkr_ref[:, 0:LANES])
        kb = _dot(h, wkr_ref[:, LANES:2 * LANES])
        kr = (ka * cos + kb * sin).astype(BF16)
        for hd in range(MLA_HEADS):
            lo = hd * HEAD_SLOT
            src = slice(hd * QK_NOPE, (hd + 1) * QK_NOPE)
            pair = hd % n_rope_cols
            rope = qr[:, pair * LANES:(pair + 1) * LANES]
            keep = low_lanes if hd < n_rope_cols else ~low_lanes
            q_ref[rows, lo:lo + QK_NOPE] = qn[:, src]
            q_ref[rows, lo + QK_NOPE:lo + HEAD_SLOT] = jnp.where(keep, rope, jnp.zeros_like(rope))
            k_ref[rows, lo:lo + QK_NOPE] = kn[:, src]
            k_ref[rows, lo + QK_NOPE:lo + HEAD_SLOT] = kr

    def sgu_part(sb, h):
        rows = slice(sb * FRONT_SUB, (sb + 1) * FRONT_SUB)
        u = jax.nn.gelu(_dot(h, wuv_ref[:, 0:SGU_OUT]))
        vv = jax.nn.gelu(_dot(h, wuv_ref[:, SGU_OUT:2 * SGU_OUT]))
        for g in range(SGU_GROUPS):
            lo = g * SGU_CH
            vg = vv[:, lo:lo + SGU_CH]
            mu = jnp.mean(vg, axis=-1, keepdims=True)
            d = vg - mu
            var = jnp.mean(d * d, axis=-1, keepdims=True)
            vn_buf[rows, lo:lo + SGU_CH] = (d * lax.rsqrt(var + EPS)
                                            * vng_ref[:, lo:lo + SGU_CH]).astype(BF16)
        for g in range(SGU_GROUPS):
            lo = g * SGU_CH
            for c in range(FRONT_SUB // CHUNK):
                t0 = sb * FRONT_SUB + c * CHUNK
                y = _dot(sw_buf[g], vn_buf[t0:t0 + CHUNK, lo:lo + SGU_CH])
                s_buf[t0:t0 + CHUNK, lo:lo + SGU_CH] = y + sgub_ref[:, lo:lo + SGU_CH]
        s_ref[rows, :] = _rms(u * s_buf[rows, :], sgn_ref[...]).astype(BF16)

    subs = range(x_ref.shape[0] // FRONT_SUB)
    hs = [normed(sb) for sb in subs]
    for sb in subs:
        sgu_part(sb, hs[sb])
    for sb in subs:
        mla_part(sb, hs[sb])
    _run_cast_riders(rider_in, rider_out)


def _even_front(x, pos, invf, sign, g, wlat, wuv, wkr, qg, wuq, kvg, wukv, vng, sguw, sgub, sgn,
                riders):
    T = x.shape[0]
    tm = FRONT_ROWS
    steps = T // tm
    row = lambda w: pl.BlockSpec((tm, w), lambda i: (i, 0))
    out_widths = (MLA_HEADS * HEAD_SLOT, MLA_HEADS * HEAD_SLOT, MLA_OUT, SGU_OUT)
    r_in, r_out, r_shapes = _cast_rider_specs(riders, steps, lambda i: i)
    consts = (invf, sign, g, wlat, wuv, wkr, qg, wuq, kvg, wukv, vng, sguw, sgub, sgn)
    return pl.pallas_call(
        functools.partial(_even_front_kernel, n_riders=len(riders)),
        grid=(steps,),
        in_specs=[row(D_MODEL), row(1)] + [_const_spec(c.shape) for c in consts] + r_in,
        out_specs=[row(w) for w in out_widths] + r_out,
        out_shape=[jax.ShapeDtypeStruct((T, w), BF16) for w in out_widths] + r_shapes,
        scratch_shapes=[pltpu.VMEM((tm, SGU_OUT), BF16), pltpu.VMEM((tm, SGU_OUT), F32),
                        pltpu.VMEM((SGU_GROUPS, CHUNK, CHUNK), BF16)],
        compiler_params=pltpu.CompilerParams(dimension_semantics=("arbitrary",),
                                             vmem_limit_bytes=VMEM_LIMIT_BYTES),
        name="even_front",
    )(x, pos, *consts, *[w for w, _ in riders])


def _rope_rows():
    lane = jnp.arange(LANES)
    inv_freq = ROPE_BASE ** (-jnp.arange(0, QK_ROPE, 2, dtype=F32) / QK_ROPE)
    invf = inv_freq[lane % HALF_ROPE][None, :]
    sign = jnp.where((lane % QK_ROPE) < HALF_ROPE, -1.0, 1.0).astype(F32)[None, :]
    return invf, sign


def _attn_kernel(*refs, n_riders):
    (q_ref, k_ref, v_ref), rider_in, (o_ref,), rider_out, (v_ext,) = _split_refs(refs, 3, 1, n_riders)
    _run_cast_riders(rider_in, rider_out)
    seq = q_ref.shape[0]
    tq = ATTN_BLOCK
    row = lax.broadcasted_iota(jnp.int32, (tq, tq), 0)
    col = lax.broadcasted_iota(jnp.int32, (tq, tq), 1)
    causal = col <= row
    v_ext[:, 0:V_HEAD] = v_ref[...]
    v_ext[:, V_HEAD:2 * V_HEAD] = jnp.ones((seq, V_HEAD), BF16)

    def scores(qi):
        r0 = qi * tq
        q = q_ref[r0:r0 + tq, :]
        sd = jnp.where(causal, _dot_nt(q, k_ref[r0:r0 + tq, :]), -jnp.inf)
        sp = _dot_nt(q, k_ref[0:r0, :]) if qi else None
        return sd, sp

    order = list(reversed(range(seq // tq)))
    ahead = [scores(qi) for qi in order[:ATTN_AHEAD]]
    for n, qi in enumerate(order):
        r0 = qi * tq
        r1 = r0 + tq
        sd, sp = ahead.pop(0)
        if n + ATTN_AHEAD < len(order):
            ahead.append(scores(order[n + ATTN_AHEAD]))
        m = jnp.max(sd, axis=-1, keepdims=True)
        if qi:
            m = jnp.maximum(m, jnp.max(sp, axis=-1, keepdims=True))
        acc = _dot(jnp.exp2(sd - m).astype(BF16), v_ext[r0:r1, :])
        if qi:
            acc = acc + _dot(jnp.exp2(sp - m).astype(BF16), v_ext[0:r0, :])
        o_ref[r0:r1, :] = (acc[:, 0:V_HEAD] / acc[:, V_HEAD:2 * V_HEAD]).astype(BF16)


def _attention(q, k, v, batch, seq, riders):
    T = q.shape[0]
    spec = lambda w: pl.BlockSpec((seq, w), lambda b, h: (b, h))
    r_in, r_out, r_shapes = _cast_rider_specs(riders, batch * MLA_HEADS,
                                              lambda b, h: b * MLA_HEADS + h)
    return pl.pallas_call(
        functools.partial(_attn_kernel, n_riders=len(riders)),
        grid=(batch, MLA_HEADS),
        in_specs=[spec(HEAD_SLOT), spec(HEAD_SLOT), spec(V_HEAD)] + r_in,
        out_specs=[spec(V_HEAD)] + r_out,
        out_shape=[jax.ShapeDtypeStruct((T, MLA_OUT), BF16)] + r_shapes,
        scratch_shapes=[pltpu.VMEM((seq, 2 * V_HEAD), BF16)],
        compiler_params=pltpu.CompilerParams(dimension_semantics=("parallel", "parallel"),
                                             vmem_limit_bytes=VMEM_LIMIT_BYTES),
        name="mla_attention",
    )(q, k, v, *[w for w, _ in riders])


def _out_mlp_kernel(x_ref, ma_ref, mb_ref, wo_ref, g_ref, w1_hbm, w2_hbm, fg_ref, *rest,
                    norm_a, final_norm):
    if norm_a:
        ag_ref, o_ref, h_buf, w1_buf, w2_buf, sem = rest
    else:
        o_ref, h_buf, w1_buf, w2_buf, sem = rest
    i = pl.program_id(0)
    half = ma_ref.shape[1]
    tf = w1_buf.shape[2]
    n_chunks = w1_hbm.shape[1] // tf
    assert n_chunks % MLP_SLOTS == 0, "chunk 0 must land in slot 0 of every row tile"

    def chunk_copies(k):
        slot = k % MLP_SLOTS
        return (pltpu.make_async_copy(w1_hbm.at[:, k * tf:(k + 1) * tf], w1_buf.at[slot],
                                      sem.at[0, slot]),
                pltpu.make_async_copy(w2_hbm.at[k * tf:(k + 1) * tf, :], w2_buf.at[slot],
                                      sem.at[1, slot]))

    @pl.when(i == 0)
    def _():
        for c in chunk_copies(0):
            c.start()

    for sb in range(x_ref.shape[0] // MLP_SUB):
        rows = slice(sb * MLP_SUB, (sb + 1) * MLP_SUB)
        ma = ma_ref[rows, :]
        if norm_a:
            ma = _rms(ma.astype(F32), ag_ref[...]).astype(BF16)
        x1 = (x_ref[rows, :] + _dot(ma, wo_ref[0:half, :])
              + _dot(mb_ref[rows, :], wo_ref[half:2 * half, :]))
        o_ref[rows, :] = x1
        h_buf[rows, :] = _rms(x1, g_ref[...]).astype(BF16)

    for k in range(n_chunks):
        slot = k % MLP_SLOTS
        for c in chunk_copies((k + 1) % n_chunks):
            c.start()
        for c in chunk_copies(k):
            c.wait()
        a = jnp.maximum(_dot(h_buf[...], w1_buf[slot]), 0.0)
        o_ref[...] += _dot((a * a).astype(BF16), w2_buf[slot])

    if final_norm:
        o_ref[...] = _rms(o_ref[...], fg_ref[...])

    @pl.when(i == pl.num_programs(0) - 1)
    def _():
        for c in chunk_copies(0):
            c.wait()


def _out_mlp(x, ma, mb, wo, g, w1, w2, fg, a_gain, *, final_norm):
    T = x.shape[0]
    tm, tf = MLP_ROWS, MLP_FF
    half = wo.shape[0] // 2
    norm_a = a_gain is not None
    in_specs = [pl.BlockSpec((tm, D_MODEL), lambda i: (i, 0)),
                pl.BlockSpec((tm, half), lambda i: (i, 0)),
                pl.BlockSpec((tm, half), lambda i: (i, mb.shape[1] // half - 1)),
                _const_spec(wo.shape), _const_spec(g.shape),
                pl.BlockSpec(memory_space=pl.ANY), pl.BlockSpec(memory_space=pl.ANY),
                _const_spec(fg.shape)]
    args = [x, ma, mb, wo, g, w1, w2, fg]
    if norm_a:
        in_specs.append(_const_spec(a_gain.shape))
        args.append(a_gain)
    return pl.pallas_call(
        functools.partial(_out_mlp_kernel, norm_a=norm_a, final_norm=final_norm),
        grid=(T // tm,),
        in_specs=in_specs,
        out_specs=pl.BlockSpec((tm, D_MODEL), lambda i: (i, 0)),
        out_shape=jax.ShapeDtypeStruct((T, D_MODEL), F32),
        scratch_shapes=[pltpu.VMEM((tm, D_MODEL), BF16),
                        pltpu.VMEM((MLP_SLOTS, D_MODEL, tf), BF16),
                        pltpu.VMEM((MLP_SLOTS, tf, D_MODEL), BF16),
                        pltpu.SemaphoreType.DMA((2, MLP_SLOTS))],
        compiler_params=pltpu.CompilerParams(dimension_semantics=("arbitrary",),
                                             vmem_limit_bytes=VMEM_LIMIT_BYTES),
        name="out_proj_mlp",
    )(*args)


def _conv_front_kernel(*refs, n_riders, tiles_per_seq):
    ins, rider_in, (o_ref,), rider_out, (hist,) = _split_refs(refs, 4, 1, n_riders)
    x_ref, g_ref, w_ref, cw_ref = ins
    tm = x_ref.shape[0]
    tn = CONV_COLS

    @pl.when(pl.program_id(0) == 0)
    def _():
        hist[...] = jnp.zeros_like(hist)

    seq_start = pl.program_id(0) % tiles_per_seq == 0
    h = _rms(x_ref[...], g_ref[...]).astype(BF16)
    row = lax.broadcasted_iota(jnp.int32, (tm, tn), 0)
    for cb in range(CONV_DIM // tn):
        c0 = cb * tn
        z = (_dot(h, w_ref[:, CONV_DIM + c0:CONV_DIM + c0 + tn])
             * _dot(h, w_ref[:, 2 * CONV_DIM + c0:2 * CONV_DIM + c0 + tn]))
        p1 = jnp.where(seq_start, 0.0, hist[7:8, c0:c0 + tn])
        p2 = jnp.where(seq_start, 0.0, hist[6:7, c0:c0 + tn])
        z1 = jnp.where(row == 0, p1, pltpu.roll(z, 1, 0))
        z2 = jnp.where(row == 0, p2, jnp.where(row == 1, p1, pltpu.roll(z, 2, 0)))
        conv = (cw_ref[0:1, c0:c0 + tn] * z2 + cw_ref[1:2, c0:c0 + tn] * z1
                + cw_ref[2:3, c0:c0 + tn] * z)
        o_ref[:, c0:c0 + tn] = (_dot(h, w_ref[:, c0:c0 + tn]) * conv).astype(BF16)
        hist[:, c0:c0 + tn] = z[tm - 8:tm, :]
    _run_cast_riders(rider_in, rider_out)


def _conv_front(x, g, w_in, conv_w, layer, seq, riders):
    T = x.shape[0]
    tm = CONV_ROWS
    steps = T // tm
    r_in, r_out, r_shapes = _cast_rider_specs(riders, steps, lambda i: i)
    return pl.pallas_call(
        functools.partial(_conv_front_kernel, n_riders=len(riders), tiles_per_seq=seq // tm),
        grid=(steps,),
        in_specs=[pl.BlockSpec((tm, D_MODEL), lambda i: (i, 0)), _const_spec(g.shape),
                  _const_spec(w_in.shape), _layer_spec(conv_w, layer)] + r_in,
        out_specs=[pl.BlockSpec((tm, CONV_DIM), lambda i: (i, 0))] + r_out,
        out_shape=[jax.ShapeDtypeStruct((T, CONV_DIM), BF16)] + r_shapes,
        scratch_shapes=[pltpu.VMEM((8, CONV_DIM), F32)],
        compiler_params=pltpu.CompilerParams(dimension_semantics=("arbitrary",),
                                             vmem_limit_bytes=VMEM_LIMIT_BYTES),
        name="conv_front",
    )(x, g, w_in, conv_w, *[w for w, _ in riders])


def _even_weights(w_in, w_uq, w_ukv):
    c1 = Q_LORA
    c2 = c1 + KV_LORA
    c3 = c2 + QK_ROPE
    k1 = w_in[:, c2:c2 + HALF_ROPE]
    k2 = w_in[:, c2 + HALF_ROPE:c3]
    wlat = w_in[:, :c2]
    wuv = w_in[:, c3:]
    wkr = jnp.concatenate([k1, k2, k1, k2, k2, k1, k2, k1], axis=1)
    wq = w_uq.reshape(Q_LORA, MLA_HEADS, QK_NOPE + QK_ROPE)
    r1 = wq[:, :, QK_NOPE:QK_NOPE + HALF_ROPE]
    r2 = wq[:, :, QK_NOPE + HALF_ROPE:]
    pairs = LANES // QK_ROPE

    def rope_cols(a, b):
        w = jnp.concatenate([a, b], axis=-1).reshape(Q_LORA, pairs, MLA_HEADS // pairs, QK_ROPE)
        return w.transpose(0, 2, 1, 3).reshape(Q_LORA, -1)

    wuq = jnp.concatenate([wq[:, :, :QK_NOPE].reshape(Q_LORA, -1), rope_cols(r1, r2),
                           rope_cols(r2, r1)], axis=1)
    wkv = w_ukv.reshape(KV_LORA, MLA_HEADS, QK_NOPE + V_HEAD)
    wukv = jnp.concatenate([wkv[:, :, :QK_NOPE].reshape(KV_LORA, -1),
                            wkv[:, :, QK_NOPE:].reshape(KV_LORA, -1)], axis=1)
    return tuple(w.astype(BF16) for w in (wlat, wuv, wkr, wuq, wukv))


def kernel(x, positions, e_norm_mix, e_w_in, e_q_norm, e_w_uq, e_kv_norm, e_w_ukv, e_v_norm,
           e_sgu_w, e_sgu_b, e_mla_out_norm, e_sgu_out_norm, e_w_out, o_norm_mix, o_w_in,
           o_conv_w, o_w_out, mlp_norm, mlp_w1, mlp_w2, final_norm):
    batch, seq, d = x.shape
    T = batch * seq
    depth = mlp_norm.shape[0]
    xf = x.reshape(T, d)
    pos = positions.reshape(T, 1)
    invf, sign = _rope_rows()
    fg = final_norm[None, :]
    row2d = lambda a: a.reshape(1, -1)

    o_wi = None
    for layer in range(depth):
        i = layer // 2
        mlp_riders = [(mlp_w1, layer), (mlp_w2, layer)]
        if layer % 2 == 0:
            wlat, wuv, wkr, wuq, wukv = _even_weights(e_w_in[i], e_w_uq[i], e_w_ukv[i])
            sgub = jnp.repeat(e_sgu_b[i].T, SGU_CH, axis=1)
            q, k, v, sn, wo = _even_front(
                xf, pos, invf, sign, row2d(e_norm_mix[i]), wlat, wuv, wkr, row2d(e_q_norm[i]), wuq,
                row2d(e_kv_norm[i]), wukv, row2d(e_v_norm[i]), e_sgu_w[i], sgub,
                row2d(e_sgu_out_norm[i]), riders=[(e_w_out, i)])
            next_in = [(o_w_in, (layer + 1) // 2)] if layer + 1 < depth else []
            ma, w1, w2, *rest = _attention(q, k, v, batch, seq, mlp_riders + next_in)
            o_wi = rest[0] if rest else None
            mb, a_gain = sn, row2d(e_mla_out_norm[i])
        else:
            ma, wo, w1, w2 = _conv_front(xf, row2d(o_norm_mix[i]), o_wi, o_conv_w, i, seq,
                                         riders=[(o_w_out, i)] + mlp_riders)
            mb, a_gain = ma, None
        xf = _out_mlp(xf, ma, mb, wo, row2d(mlp_norm[layer]), w1, w2, fg, a_gain,
                      final_norm=(layer == depth - 1))
    return xf.reshape(batch, seq, d)
```

```python
import functools
import math

import jax
import jax.numpy as jnp
from jax import lax
from jax.experimental import pallas as pl
from jax.experimental.pallas import tpu as pltpu

D_MODEL = 2048
MLA_HEADS = 8
Q_LORA = 512
KV_LORA = 512
QK_NOPE = 128
QK_ROPE = 64
V_HEAD = 128
ROPE_BASE = 10000.0
SGU_GROUPS = 8
SGU_CH = 128
CHUNK = 128
CONV_DIM = D_MODEL
CONV_WIDTH = 3
D_FF = 4 * D_MODEL
EPS = 1e-6
MLA_OUT = MLA_HEADS * V_HEAD
SGU_OUT = SGU_GROUPS * SGU_CH
HALF_ROPE = QK_ROPE // 2
LOG2_E = math.log2(math.e)
HEAD_SLOT = 2 * QK_NOPE

LANES = 128
BF16_ROWS = 16
VMEM_LIMIT_BYTES = 60 * 1024 * 1024

FRONT_ROWS = 512
FRONT_SUB = 256
ATTN_BLOCK = 256
ATTN_AHEAD = 2
MLP_ROWS = 512
MLP_SUB = 256
MLP_FF = 1024
MLP_SLOTS = 2
CONV_ROWS = 512
CONV_COLS = 512

BF16 = jnp.bfloat16
F32 = jnp.float32


def _dot(a, b):
    return jnp.dot(a, b, preferred_element_type=F32)


def _dot_nt(a, b):
    return lax.dot_general(a, b, (((1,), (1,)), ((), ())), preferred_element_type=F32)


def _rms(x, g):
    return x * lax.rsqrt(jnp.mean(x * x, axis=-1, keepdims=True) + EPS) * g


def _const_spec(shape):
    return pl.BlockSpec(shape, lambda *_: (0,) * len(shape), pipeline_mode=pl.Buffered(1))


def _layer_spec(w, layer):
    return pl.BlockSpec((None,) + w.shape[1:], lambda *_: (layer, 0, 0), pipeline_mode=pl.Buffered(1))


def _cast_rider_specs(riders, steps, step_of):
    ins, outs, shapes = [], [], []
    for w, layer in riders:
        _, rows, cols = w.shape
        rb = rows // steps
        assert rb * steps == rows and rb % BF16_ROWS == 0, (w.shape, steps)
        ins.append(pl.BlockSpec((None, rb, cols), lambda *g, layer=layer: (layer, step_of(*g), 0)))
        outs.append(pl.BlockSpec((rb, cols), lambda *g: (step_of(*g), 0)))
        shapes.append(jax.ShapeDtypeStruct((rows, cols), BF16))
    return ins, outs, shapes


def _split_refs(refs, n_in, n_out, n_riders):
    cuts = [n_in, n_in + n_riders, n_in + n_riders + n_out, n_in + 2 * n_riders + n_out]
    return (refs[:cuts[0]], refs[cuts[0]:cuts[1]], refs[cuts[1]:cuts[2]], refs[cuts[2]:cuts[3]],
            refs[cuts[3]:])


def _run_cast_riders(srcs, dsts):
    for src, dst in zip(srcs, dsts):
        dst[...] = src[...].astype(BF16)


def _rope_tables(pos_ref, r0, invf, lane_group):
    n = LANES // HALF_ROPE
    seg = lane_group.shape[0]
    packed = jnp.zeros((seg, LANES), F32)
    for j in range(n):
        p = pos_ref[r0 + j * seg:r0 + (j + 1) * seg, :].astype(F32)
        packed = jnp.where(lane_group == j, p, packed)
    ang = packed * invf
    tables = []
    for t in (jnp.cos(ang), jnp.sin(ang)):
        parts = []
        for j in range(n):
            one = jnp.where(lane_group == j, t, 0.0)
            full = one
            for s in range(1, n):
                full = full + pltpu.roll(one, s * HALF_ROPE, 1)
            parts.append(full)
        tables.append(jnp.concatenate(parts, axis=0))
    return tables


def _even_front_kernel(*refs, n_riders):
    ins, rider_in, outs, rider_out, (vn_buf, s_buf, sw_buf) = _split_refs(refs, 16, 4, n_riders)
    (x_ref, pos_ref, invf_ref, sign_ref, g_ref, wlat_ref, wuv_ref, wkr_ref, qg_ref, wuq_ref,
     kvg_ref, wukv_ref, vng_ref, sguw_ref, sgub_ref, sgn_ref) = ins
    q_ref, k_ref, v_ref, s_ref = outs

    @pl.when(pl.program_id(0) == 0)
    def _():
        row = lax.broadcasted_iota(jnp.int32, (CHUNK, CHUNK), 0)
        col = lax.broadcasted_iota(jnp.int32, (CHUNK, CHUNK), 1)
        for g in range(SGU_GROUPS):
            sw_buf[g] = jnp.where(col <= row, sguw_ref[g], 0.0).astype(BF16)

    scale = (QK_NOPE + QK_ROPE) ** -0.5 * LOG2_E
    n_rope_cols = MLA_HEADS * QK_ROPE // LANES
    low_lanes = lax.broadcasted_iota(jnp.int32, (FRONT_SUB, LANES), 1) < QK_ROPE
    seg = FRONT_SUB // (LANES // HALF_ROPE)
    lane_group = lax.broadcasted_iota(jnp.int32, (seg, LANES), 1) // HALF_ROPE

    def normed(sb):
        return _rms(x_ref[sb * FRONT_SUB:(sb + 1) * FRONT_SUB, :], g_ref[...]).astype(BF16)

    def first_dots(h):
        return dict(
            cq=_dot(h, wlat_ref[:, 0:Q_LORA]),
            ckv=_dot(h, wlat_ref[:, Q_LORA:Q_LORA + KV_LORA]),
            ka=_dot(h, wkr_ref[:, 0:LANES]),
            kb=_dot(h, wkr_ref[:, LANES:2 * LANES]),
            u=_dot(h, wuv_ref[:, 0:SGU_OUT]),
            v=_dot(h, wuv_ref[:, SGU_OUT:2 * SGU_OUT]))

    def mla_part(sb, raw):
        rows = slice(sb * FRONT_SUB, (sb + 1) * FRONT_SUB)
        cos, sin = _rope_tables(pos_ref, sb * FRONT_SUB, invf_ref[...], lane_group)
        sin = sin * sign_ref[...]

        cq = _rms(raw["cq"], qg_ref[...]).astype(BF16)
        q = _dot(cq, wuq_ref[...])
        cos_q = jnp.concatenate([cos] * n_rope_cols, axis=1)
        sin_q = jnp.concatenate([sin] * n_rope_cols, axis=1)
        r0 = MLA_OUT
        r1 = r0 + n_rope_cols * LANES
        qn = (q[:, :MLA_OUT] * scale).astype(BF16)
        qr = ((q[:, r0:r1] * cos_q + q[:, r1:] * sin_q) * scale).astype(BF16)

        ckv = _rms(raw["ckv"], kvg_ref[...]).astype(BF16)
        kv = _dot(ckv, wukv_ref[...])
        kn = kv[:, :MLA_OUT].astype(BF16)
        v_ref[rows, :] = kv[:, MLA_OUT:].astype(BF16)

        kr = (raw["ka"] * cos + raw["kb"] * sin).astype(BF16)
        for hd in range(MLA_HEADS):
            lo = hd * HEAD_SLOT
            src = slice(hd * QK_NOPE, (hd + 1) * QK_NOPE)
            pair = hd % n_rope_cols
            rope = qr[:, pair * LANES:(pair + 1) * LANES]
            keep = low_lanes if hd < n_rope_cols else ~low_lanes
            q_ref[rows, lo:lo + QK_NOPE] = qn[:, src]
            q_ref[rows, lo + QK_NOPE:lo + HEAD_SLOT] = jnp.where(keep, rope, jnp.zeros_like(rope))
            k_ref[rows, lo:lo + QK_NOPE] = kn[:, src]
            k_ref[rows, lo + QK_NOPE:lo + HEAD_SLOT] = kr

    def sgu_norm(sb, raw):
        rows = slice(sb * FRONT_SUB, (sb + 1) * FRONT_SUB)
        u = jax.nn.gelu(raw["u"])
        vv = jax.nn.gelu(raw["v"])
        for g in range(SGU_GROUPS):
            lo = g * SGU_CH
            vg = vv[:, lo:lo + SGU_CH]
            mu = jnp.mean(vg, axis=-1, keepdims=True)
            d = vg - mu
            var = jnp.mean(d * d, axis=-1, keepdims=True)
            vn_buf[rows, lo:lo + SGU_CH] = (d * lax.rsqrt(var + EPS)
                                            * vng_ref[:, lo:lo + SGU_CH]).astype(BF16)
        return u

    def sgu_mix(sb, u):
        rows = slice(sb * FRONT_SUB, (sb + 1) * FRONT_SUB)
        for g in range(SGU_GROUPS):
            lo = g * SGU_CH
            for c in range(FRONT_SUB // CHUNK):
                t0 = sb * FRONT_SUB + c * CHUNK
                y = _dot(sw_buf[g], vn_buf[t0:t0 + CHUNK, lo:lo + SGU_CH])
                s_buf[t0:t0 + CHUNK, lo:lo + SGU_CH] = y + sgub_ref[:, lo:lo + SGU_CH]
        s_ref[rows, :] = _rms(u * s_buf[rows, :], sgn_ref[...]).astype(BF16)

    subs = list(range(x_ref.shape[0] // FRONT_SUB))
    raw = first_dots(normed(subs[0]))
    for sb in subs:
        nxt = first_dots(normed(sb + 1)) if sb + 1 < len(subs) else None
        mla_part(sb, raw)
        sgu_mix(sb, sgu_norm(sb, raw))
        raw = nxt
    _run_cast_riders(rider_in, rider_out)


def _even_front(x, pos, invf, sign, g, wlat, wuv, wkr, qg, wuq, kvg, wukv, vng, sguw, sgub, sgn,
                riders):
    T = x.shape[0]
    tm = FRONT_ROWS
    steps = T // tm
    row = lambda w: pl.BlockSpec((tm, w), lambda i: (i, 0))
    out_widths = (MLA_HEADS * HEAD_SLOT, MLA_HEADS * HEAD_SLOT, MLA_OUT, SGU_OUT)
    r_in, r_out, r_shapes = _cast_rider_specs(riders, steps, lambda i: i)
    consts = (invf, sign, g, wlat, wuv, wkr, qg, wuq, kvg, wukv, vng, sguw, sgub, sgn)
    return pl.pallas_call(
        functools.partial(_even_front_kernel, n_riders=len(riders)),
        grid=(steps,),
        in_specs=[row(D_MODEL), row(1)] + [_const_spec(c.shape) for c in consts] + r_in,
        out_specs=[row(w) for w in out_widths] + r_out,
        out_shape=[jax.ShapeDtypeStruct((T, w), BF16) for w in out_widths] + r_shapes,
        scratch_shapes=[pltpu.VMEM((tm, SGU_OUT), BF16), pltpu.VMEM((tm, SGU_OUT), F32),
                        pltpu.VMEM((SGU_GROUPS, CHUNK, CHUNK), BF16)],
        compiler_params=pltpu.CompilerParams(dimension_semantics=("arbitrary",),
                                             vmem_limit_bytes=VMEM_LIMIT_BYTES),
        name="even_front",
    )(x, pos, *consts, *[w for w, _ in riders])


def _rope_rows():
    lane = jnp.arange(LANES)
    inv_freq = ROPE_BASE ** (-jnp.arange(0, QK_ROPE, 2, dtype=F32) / QK_ROPE)
    invf = inv_freq[lane % HALF_ROPE][None, :]
    sign = jnp.where((lane % QK_ROPE) < HALF_ROPE, -1.0, 1.0).astype(F32)[None, :]
    return invf, sign


def _attn_kernel(*refs, n_riders):
    (q_ref, k_ref, v_ref), rider_in, (o_ref,), rider_out, (v_ext,) = _split_refs(refs, 3, 1, n_riders)
    _run_cast_riders(rider_in, rider_out)
    seq = q_ref.shape[0]
    tq = ATTN_BLOCK
    row = lax.broadcasted_iota(jnp.int32, (tq, tq), 0)
    col = lax.broadcasted_iota(jnp.int32, (tq, tq), 1)
    causal = col <= row
    v_ext[:, 0:V_HEAD] = v_ref[...]
    v_ext[:, V_HEAD:2 * V_HEAD] = jnp.ones((seq, V_HEAD), BF16)

    def scores(qi):
        r0 = qi * tq
        q = q_ref[r0:r0 + tq, :]
        sd = jnp.where(causal, _dot_nt(q, k_ref[r0:r0 + tq, :]), -jnp.inf)
        sp = _dot_nt(q, k_ref[0:r0, :]) if qi else None
        return sd, sp

    order = list(reversed(range(seq // tq)))
    ahead = [scores(qi) for qi in order[:ATTN_AHEAD]]
    for n, qi in enumerate(order):
        r0 = qi * tq
        r1 = r0 + tq
        sd, sp = ahead.pop(0)
        if n + ATTN_AHEAD < len(order):
            ahead.append(scores(order[n + ATTN_AHEAD]))
        m = jnp.max(sd, axis=-1, keepdims=True)
        if qi:
            m = jnp.maximum(m, jnp.max(sp, axis=-1, keepdims=True))
        acc = _dot(jnp.exp2(sd - m).astype(BF16), v_ext[r0:r1, :])
        if qi:
            acc = acc + _dot(jnp.exp2(sp - m).astype(BF16), v_ext[0:r0, :])
        o_ref[r0:r1, :] = (acc[:, 0:V_HEAD] / acc[:, V_HEAD:2 * V_HEAD]).astype(BF16)


def _attention(q, k, v, batch, seq, riders):
    T = q.shape[0]
    spec = lambda w: pl.BlockSpec((seq, w), lambda b, h: (b, h))
    r_in, r_out, r_shapes = _cast_rider_specs(riders, batch * MLA_HEADS,
                                              lambda b, h: b * MLA_HEADS + h)
    return pl.pallas_call(
        functools.partial(_attn_kernel, n_riders=len(riders)),
        grid=(batch, MLA_HEADS),
        in_specs=[spec(HEAD_SLOT), spec(HEAD_SLOT), spec(V_HEAD)] + r_in,
        out_specs=[spec(V_HEAD)] + r_out,
        out_shape=[jax.ShapeDtypeStruct((T, MLA_OUT), BF16)] + r_shapes,
        scratch_shapes=[pltpu.VMEM((seq, 2 * V_HEAD), BF16)],
        compiler_params=pltpu.CompilerParams(dimension_semantics=("parallel", "parallel"),
                                             vmem_limit_bytes=VMEM_LIMIT_BYTES),
        name="mla_attention",
    )(q, k, v, *[w for w, _ in riders])


def _out_mlp_kernel(x_ref, ma_ref, mb_ref, wo_ref, g_ref, w1_hbm, w2_hbm, fg_ref, *rest,
                    norm_a, final_norm):
    if norm_a:
        ag_ref, o_ref, h_buf, w1_buf, w2_buf, sem = rest
    else:
        o_ref, h_buf, w1_buf, w2_buf, sem = rest
    i = pl.program_id(0)
    half = ma_ref.shape[1]
    tf = w1_buf.shape[2]
    n_chunks = w1_hbm.shape[1] // tf
    assert n_chunks % MLP_SLOTS == 0, "chunk 0 must land in slot 0 of every row tile"

    def chunk_copies(k):
        slot = k % MLP_SLOTS
        return (pltpu.make_async_copy(w1_hbm.at[:, k * tf:(k + 1) * tf], w1_buf.at[slot],
                                      sem.at[0, slot]),
                pltpu.make_async_copy(w2_hbm.at[k * tf:(k + 1) * tf, :], w2_buf.at[slot],
                                      sem.at[1, slot]))

    @pl.when(i == 0)
    def _():
        for c in chunk_copies(0):
            c.start()

    for sb in range(x_ref.shape[0] // MLP_SUB):
        rows = slice(sb * MLP_SUB, (sb + 1) * MLP_SUB)
        ma = ma_ref[rows, :]
        if norm_a:
            ma = _rms(ma.astype(F32), ag_ref[...]).astype(BF16)
        x1 = (x_ref[rows, :] + _dot(ma, wo_ref[0:half, :])
              + _dot(mb_ref[rows, :], wo_ref[half:2 * half, :]))
        o_ref[rows, :] = x1
        h_buf[rows, :] = _rms(x1, g_ref[...]).astype(BF16)

    for k in range(n_chunks):
        slot = k % MLP_SLOTS
        for c in chunk_copies((k + 1) % n_chunks):
            c.start()
        for c in chunk_copies(k):
            c.wait()
        a = jnp.maximum(_dot(h_buf[...], w1_buf[slot]), 0.0)
        o_ref[...] += _dot((a * a).astype(BF16), w2_buf[slot])

    if final_norm:
        o_ref[...] = _rms(o_ref[...], fg_ref[...])

    @pl.when(i == pl.num_programs(0) - 1)
    def _():
        for c in chunk_copies(0):
            c.wait()


def _out_mlp(x, ma, mb, wo, g, w1, w2, fg, a_gain, *, final_norm):
    T = x.shape[0]
    tm, tf = MLP_ROWS, MLP_FF
    half = wo.shape[0] // 2
    norm_a = a_gain is not None
    in_specs = [pl.BlockSpec((tm, D_MODEL), lambda i: (i, 0)),
                pl.BlockSpec((tm, half), lambda i: (i, 0)),
                pl.BlockSpec((tm, half), lambda i: (i, mb.shape[1] // half - 1)),
                _const_spec(wo.shape), _const_spec(g.shape),
                pl.BlockSpec(memory_space=pl.ANY), pl.BlockSpec(memory_space=pl.ANY),
                _const_spec(fg.shape)]
    args = [x, ma, mb, wo, g, w1, w2, fg]
    if norm_a:
        in_specs.append(_const_spec(a_gain.shape))
        args.append(a_gain)
    return pl.pallas_call(
        functools.partial(_out_mlp_kernel, norm_a=norm_a, final_norm=final_norm),
        grid=(T // tm,),
        in_specs=in_specs,
        out_specs=pl.BlockSpec((tm, D_MODEL), lambda i: (i, 0)),
        out_shape=jax.ShapeDtypeStruct((T, D_MODEL), F32),
        scratch_shapes=[pltpu.VMEM((tm, D_MODEL), BF16),
                        pltpu.VMEM((MLP_SLOTS, D_MODEL, tf), BF16),
                        pltpu.VMEM((MLP_SLOTS, tf, D_MODEL), BF16),
                        pltpu.SemaphoreType.DMA((2, MLP_SLOTS))],
        compiler_params=pltpu.CompilerParams(dimension_semantics=("arbitrary",),
                                             vmem_limit_bytes=VMEM_LIMIT_BYTES),
        name="out_proj_mlp",
    )(*args)


def _conv_front_kernel(*refs, n_riders, tiles_per_seq):
    ins, rider_in, (o_ref,), rider_out, (hist,) = _split_refs(refs, 4, 1, n_riders)
    x_ref, g_ref, w_ref, cw_ref = ins
    tm = x_ref.shape[0]
    tn = CONV_COLS

    @pl.when(pl.program_id(0) == 0)
    def _():
        hist[...] = jnp.zeros_like(hist)

    seq_start = pl.program_id(0) % tiles_per_seq == 0
    h = _rms(x_ref[...], g_ref[...]).astype(BF16)
    row = lax.broadcasted_iota(jnp.int32, (tm, tn), 0)
    for cb in range(CONV_DIM // tn):
        c0 = cb * tn
        z = (_dot(h, w_ref[:, CONV_DIM + c0:CONV_DIM + c0 + tn])
             * _dot(h, w_ref[:, 2 * CONV_DIM + c0:2 * CONV_DIM + c0 + tn]))
        p1 = jnp.where(seq_start, 0.0, hist[7:8, c0:c0 + tn])
        p2 = jnp.where(seq_start, 0.0, hist[6:7, c0:c0 + tn])
        z1 = jnp.where(row == 0, p1, pltpu.roll(z, 1, 0))
        z2 = jnp.where(row == 0, p2, jnp.where(row == 1, p1, pltpu.roll(z, 2, 0)))
        conv = (cw_ref[0:1, c0:c0 + tn] * z2 + cw_ref[1:2, c0:c0 + tn] * z1
                + cw_ref[2:3, c0:c0 + tn] * z)
        o_ref[:, c0:c0 + tn] = (_dot(h, w_ref[:, c0:c0 + tn]) * conv).astype(BF16)
        hist[:, c0:c0 + tn] = z[tm - 8:tm, :]
    _run_cast_riders(rider_in, rider_out)


def _conv_front(x, g, w_in, conv_w, layer, seq, riders):
    T = x.shape[0]
    tm = CONV_ROWS
    steps = T // tm
    r_in, r_out, r_shapes = _cast_rider_specs(riders, steps, lambda i: i)
    return pl.pallas_call(
        functools.partial(_conv_front_kernel, n_riders=len(riders), tiles_per_seq=seq // tm),
        grid=(steps,),
        in_specs=[pl.BlockSpec((tm, D_MODEL), lambda i: (i, 0)), _const_spec(g.shape),
                  _const_spec(w_in.shape), _layer_spec(conv_w, layer)] + r_in,
        out_specs=[pl.BlockSpec((tm, CONV_DIM), lambda i: (i, 0))] + r_out,
        out_shape=[jax.ShapeDtypeStruct((T, CONV_DIM), BF16)] + r_shapes,
        scratch_shapes=[pltpu.VMEM((8, CONV_DIM), F32)],
        compiler_params=pltpu.CompilerParams(dimension_semantics=("arbitrary",),
                                             vmem_limit_bytes=VMEM_LIMIT_BYTES),
        name="conv_front",
    )(x, g, w_in, conv_w, *[w for w, _ in riders])


def _even_weights(w_in, w_uq, w_ukv):
    c1 = Q_LORA
    c2 = c1 + KV_LORA
    c3 = c2 + QK_ROPE
    k1 = w_in[:, c2:c2 + HALF_ROPE]
    k2 = w_in[:, c2 + HALF_ROPE:c3]
    wlat = w_in[:, :c2]
    wuv = w_in[:, c3:]
    wkr = jnp.concatenate([k1, k2, k1, k2, k2, k1, k2, k1], axis=1)
    wq = w_uq.reshape(Q_LORA, MLA_HEADS, QK_NOPE + QK_ROPE)
    r1 = wq[:, :, QK_NOPE:QK_NOPE + HALF_ROPE]
    r2 = wq[:, :, QK_NOPE + HALF_ROPE:]
    pairs = LANES // QK_ROPE

    def rope_cols(a, b):
        w = jnp.concatenate([a, b], axis=-1).reshape(Q_LORA, pairs, MLA_HEADS // pairs, QK_ROPE)
        return w.transpose(0, 2, 1, 3).reshape(Q_LORA, -1)

    wuq = jnp.concatenate([wq[:, :, :QK_NOPE].reshape(Q_LORA, -1), rope_cols(r1, r2),
                           rope_cols(r2, r1)], axis=1)
    wkv = w_ukv.reshape(KV_LORA, MLA_HEADS, QK_NOPE + V_HEAD)
    wukv = jnp.concatenate([wkv[:, :, :QK_NOPE].reshape(KV_LORA, -1),
                            wkv[:, :, QK_NOPE:].reshape(KV_LORA, -1)], axis=1)
    return tuple(w.astype(BF16) for w in (wlat, wuv, wkr, wuq, wukv))


def kernel(x, positions, e_norm_mix, e_w_in, e_q_norm, e_w_uq, e_kv_norm, e_w_ukv, e_v_norm,
           e_sgu_w, e_sgu_b, e_mla_out_norm, e_sgu_out_norm, e_w_out, o_norm_mix, o_w_in,
           o_conv_w, o_w_out, mlp_norm, mlp_w1, mlp_w2, final_norm):
    batch, seq, d = x.shape
    T = batch * seq
    depth = mlp_norm.shape[0]
    xf = x.reshape(T, d)
    pos = positions.reshape(T, 1)
    invf, sign = _rope_rows()
    fg = final_norm[None, :]
    row2d = lambda a: a.reshape(1, -1)

    o_wi = None
    for layer in range(depth):
        i = layer // 2
        mlp_riders = [(mlp_w1, layer), (mlp_w2, layer)]
        if layer % 2 == 0:
            wlat, wuv, wkr, wuq, wukv = _even_weights(e_w_in[i], e_w_uq[i], e_w_ukv[i])
            sgub = jnp.repeat(e_sgu_b[i].T, SGU_CH, axis=1)
            q, k, v, sn, wo = _even_front(
                xf, pos, invf, sign, row2d(e_norm_mix[i]), wlat, wuv, wkr, row2d(e_q_norm[i]), wuq,
                row2d(e_kv_norm[i]), wukv, row2d(e_v_norm[i]), e_sgu_w[i], sgub,
                row2d(e_sgu_out_norm[i]), riders=[(e_w_out, i)])
            next_in = [(o_w_in, (layer + 1) // 2)] if layer + 1 < depth else []
            ma, w1, w2, *rest = _attention(q, k, v, batch, seq, mlp_riders + next_in)
            o_wi = rest[0] if rest else None
            mb, a_gain = sn, row2d(e_mla_out_norm[i])
        else:
            ma, wo, w1, w2 = _conv_front(xf, row2d(o_norm_mix[i]), o_wi, o_conv_w, i, seq,
                                         riders=[(o_w_out, i)] + mlp_riders)
            mb, a_gain = ma, None
        xf = _out_mlp(xf, ma, mb, wo, row2d(mlp_norm[layer]), w1, w2, fg, a_gain,
                      final_norm=(layer == depth - 1))
    return xf.reshape(batch, seq, d)
```

```python
import functools
import math

import jax
import jax.numpy as jnp
from jax import lax
from jax.experimental import pallas as pl
from jax.experimental.pallas import tpu as pltpu

D_MODEL = 2048
MLA_HEADS = 8
Q_LORA = 512
KV_LORA = 512
QK_NOPE = 128
QK_ROPE = 64
V_HEAD = 128
ROPE_BASE = 10000.0
SGU_GROUPS = 8
SGU_CH = 128
CHUNK = 128
CONV_DIM = D_MODEL
CONV_WIDTH = 3
D_FF = 4 * D_MODEL
EPS = 1e-6
MLA_OUT = MLA_HEADS * V_HEAD
SGU_OUT = SGU_GROUPS * SGU_CH
HALF_ROPE = QK_ROPE // 2
LOG2_E = math.log2(math.e)
HEAD_SLOT = 2 * QK_NOPE

LANES = 128
BF16_ROWS = 16
VMEM_LIMIT_BYTES = 60 * 1024 * 1024

FRONT_ROWS = 512
FRONT_SUB = 256
ATTN_BLOCK = 256
ATTN_HEADS = 2
ATTN_AHEAD = 2
MLP_ROWS = 512
MLP_SUB = 256
MLP_FF = 1024
MLP_SLOTS = 2
CONV_ROWS = 512
CONV_COLS = 512

BF16 = jnp.bfloat16
F32 = jnp.float32


def _dot(a, b):
    return jnp.dot(a, b, preferred_element_type=F32)


def _dot_nt(a, b):
    return lax.dot_general(a, b, (((1,), (1,)), ((), ())), preferred_element_type=F32)


def _rms(x, g):
    return x * lax.rsqrt(jnp.mean(x * x, axis=-1, keepdims=True) + EPS) * g


def _const_spec(shape):
    return pl.BlockSpec(shape, lambda *_: (0,) * len(shape), pipeline_mode=pl.Buffered(1))


def _layer_spec(w, layer):
    return pl.BlockSpec((None,) + w.shape[1:], lambda *_: (layer, 0, 0), pipeline_mode=pl.Buffered(1))


def _cast_rider_specs(riders, steps, step_of):
    ins, outs, shapes = [], [], []
    for w, layer in riders:
        _, rows, cols = w.shape
        rb = rows // steps
        assert rb * steps == rows and rb % BF16_ROWS == 0, (w.shape, steps)
        ins.append(pl.BlockSpec((None, rb, cols), lambda *g, layer=layer: (layer, step_of(*g), 0)))
        outs.append(pl.BlockSpec((rb, cols), lambda *g: (step_of(*g), 0)))
        shapes.append(jax.ShapeDtypeStruct((rows, cols), BF16))
    return ins, outs, shapes


def _split_refs(refs, n_in, n_out, n_riders):
    cuts = [n_in, n_in + n_riders, n_in + n_riders + n_out, n_in + 2 * n_riders + n_out]
    return (refs[:cuts[0]], refs[cuts[0]:cuts[1]], refs[cuts[1]:cuts[2]], refs[cuts[2]:cuts[3]],
            refs[cuts[3]:])


def _run_cast_riders(srcs, dsts):
    for src, dst in zip(srcs, dsts):
        dst[...] = src[...].astype(BF16)


def _rope_tables(pos_ref, r0, invf, lane_group):
    n = LANES // HALF_ROPE
    seg = lane_group.shape[0]
    packed = jnp.zeros((seg, LANES), F32)
    for j in range(n):
        p = pos_ref[r0 + j * seg:r0 + (j + 1) * seg, :].astype(F32)
        packed = jnp.where(lane_group == j, p, packed)
    ang = packed * invf
    tables = []
    for t in (jnp.cos(ang), jnp.sin(ang)):
        parts = []
        for j in range(n):
            one = jnp.where(lane_group == j, t, 0.0)
            full = one
            for s in range(1, n):
                full = full + pltpu.roll(one, s * HALF_ROPE, 1)
            parts.append(full)
        tables.append(jnp.concatenate(parts, axis=0))
    return tables


def _even_front_kernel(*refs, n_riders):
    ins, rider_in, outs, rider_out, (vn_buf, s_buf, sw_buf) = _split_refs(refs, 16, 4, n_riders)
    (x_ref, pos_ref, invf_ref, sign_ref, g_ref, wlat_ref, wuv_ref, wkr_ref, qg_ref, wuq_ref,
     kvg_ref, wukv_ref, vng_ref, sguw_ref, sgub_ref, sgn_ref) = ins
    q_ref, k_ref, v_ref, s_ref = outs

    @pl.when(pl.program_id(0) == 0)
    def _():
        row = lax.broadcasted_iota(jnp.int32, (CHUNK, CHUNK), 0)
        col = lax.broadcasted_iota(jnp.int32, (CHUNK, CHUNK), 1)
        for g in range(SGU_GROUPS):
            sw_buf[g] = jnp.where(col <= row, sguw_ref[g], 0.0).astype(BF16)

    scale = (QK_NOPE + QK_ROPE) ** -0.5 * LOG2_E
    n_rope_cols = MLA_HEADS * QK_ROPE // LANES
    low_lanes = lax.broadcasted_iota(jnp.int32, (FRONT_SUB, LANES), 1) < QK_ROPE
    seg = FRONT_SUB // (LANES // HALF_ROPE)
    lane_group = lax.broadcasted_iota(jnp.int32, (seg, LANES), 1) // HALF_ROPE

    def normed(sb):
        return _rms(x_ref[sb * FRONT_SUB:(sb + 1) * FRONT_SUB, :], g_ref[...]).astype(BF16)

    def first_dots(h):
        return dict(
            cq=_dot(h, wlat_ref[:, 0:Q_LORA]),
            ckv=_dot(h, wlat_ref[:, Q_LORA:Q_LORA + KV_LORA]),
            ka=_dot(h, wkr_ref[:, 0:LANES]),
            kb=_dot(h, wkr_ref[:, LANES:2 * LANES]),
            u=_dot(h, wuv_ref[:, 0:SGU_OUT]),
            v=_dot(h, wuv_ref[:, SGU_OUT:2 * SGU_OUT]))

    def mla_part(sb, raw):
        rows = slice(sb * FRONT_SUB, (sb + 1) * FRONT_SUB)
        cos, sin = _rope_tables(pos_ref, sb * FRONT_SUB, invf_ref[...], lane_group)
        sin = sin * sign_ref[...]

        cq = _rms(raw["cq"], qg_ref[...]).astype(BF16)
        q = _dot(cq, wuq_ref[...])
        cos_q = jnp.concatenate([cos] * n_rope_cols, axis=1)
        sin_q = jnp.concatenate([sin] * n_rope_cols, axis=1)
        r0 = MLA_OUT
        r1 = r0 + n_rope_cols * LANES
        qn = (q[:, :MLA_OUT] * scale).astype(BF16)
        qr = ((q[:, r0:r1] * cos_q + q[:, r1:] * sin_q) * scale).astype(BF16)

        ckv = _rms(raw["ckv"], kvg_ref[...]).astype(BF16)
        kv = _dot(ckv, wukv_ref[...])
        kn = kv[:, :MLA_OUT].astype(BF16)
        v_ref[rows, :] = kv[:, MLA_OUT:].astype(BF16)

        kr = (raw["ka"] * cos + raw["kb"] * sin).astype(BF16)
        for hd in range(MLA_HEADS):
            lo = hd * HEAD_SLOT
            src = slice(hd * QK_NOPE, (hd + 1) * QK_NOPE)
            pair = hd % n_rope_cols
            rope = qr[:, pair * LANES:(pair + 1) * LANES]
            keep = low_lanes if hd < n_rope_cols else ~low_lanes
            q_ref[rows, lo:lo + QK_NOPE] = qn[:, src]
            q_ref[rows, lo + QK_NOPE:lo + HEAD_SLOT] = jnp.where(keep, rope, jnp.zeros_like(rope))
            k_ref[rows, lo:lo + QK_NOPE] = kn[:, src]
            k_ref[rows, lo + QK_NOPE:lo + HEAD_SLOT] = kr

    def sgu_norm(sb, raw):
        rows = slice(sb * FRONT_SUB, (sb + 1) * FRONT_SUB)
        u = jax.nn.gelu(raw["u"])
        vv = jax.nn.gelu(raw["v"])
        for g in range(SGU_GROUPS):
            lo = g * SGU_CH
            vg = vv[:, lo:lo + SGU_CH]
            mu = jnp.mean(vg, axis=-1, keepdims=True)
            d = vg - mu
            var = jnp.mean(d * d, axis=-1, keepdims=True)
            vn_buf[rows, lo:lo + SGU_CH] = (d * lax.rsqrt(var + EPS)
                                            * vng_ref[:, lo:lo + SGU_CH]).astype(BF16)
        return u

    def sgu_mix(sb, u):
        rows = slice(sb * FRONT_SUB, (sb + 1) * FRONT_SUB)
        for g in range(SGU_GROUPS):
            lo = g * SGU_CH
            for c in range(FRONT_SUB // CHUNK):
                t0 = sb * FRONT_SUB + c * CHUNK
                y = _dot(sw_buf[g], vn_buf[t0:t0 + CHUNK, lo:lo + SGU_CH])
                s_buf[t0:t0 + CHUNK, lo:lo + SGU_CH] = y + sgub_ref[:, lo:lo + SGU_CH]
        s_ref[rows, :] = _rms(u * s_buf[rows, :], sgn_ref[...]).astype(BF16)

    subs = list(range(x_ref.shape[0] // FRONT_SUB))
    raw = first_dots(normed(subs[0]))
    for sb in subs:
        nxt = first_dots(normed(sb + 1)) if sb + 1 < len(subs) else None
        mla_part(sb, raw)
        sgu_mix(sb, sgu_norm(sb, raw))
        raw = nxt
    _run_cast_riders(rider_in, rider_out)


def _even_front(x, pos, invf, sign, g, wlat, wuv, wkr, qg, wuq, kvg, wukv, vng, sguw, sgub, sgn,
                riders):
    T = x.shape[0]
    tm = FRONT_ROWS
    steps = T // tm
    row = lambda w: pl.BlockSpec((tm, w), lambda i: (i, 0))
    out_widths = (MLA_HEADS * HEAD_SLOT, MLA_HEADS * HEAD_SLOT, MLA_OUT, SGU_OUT)
    r_in, r_out, r_shapes = _cast_rider_specs(riders, steps, lambda i: i)
    consts = (invf, sign, g, wlat, wuv, wkr, qg, wuq, kvg, wukv, vng, sguw, sgub, sgn)
    return pl.pallas_call(
        functools.partial(_even_front_kernel, n_riders=len(riders)),
        grid=(steps,),
        in_specs=[row(D_MODEL), row(1)] + [_const_spec(c.shape) for c in consts] + r_in,
        out_specs=[row(w) for w in out_widths] + r_out,
        out_shape=[jax.ShapeDtypeStruct((T, w), BF16) for w in out_widths] + r_shapes,
        scratch_shapes=[pltpu.VMEM((tm, SGU_OUT), BF16), pltpu.VMEM((tm, SGU_OUT), F32),
                        pltpu.VMEM((SGU_GROUPS, CHUNK, CHUNK), BF16)],
        compiler_params=pltpu.CompilerParams(dimension_semantics=("arbitrary",),
                                             vmem_limit_bytes=VMEM_LIMIT_BYTES),
        name="even_front",
    )(x, pos, *consts, *[w for w, _ in riders])


def _rope_rows():
    lane = jnp.arange(LANES)
    inv_freq = ROPE_BASE ** (-jnp.arange(0, QK_ROPE, 2, dtype=F32) / QK_ROPE)
    invf = inv_freq[lane % HALF_ROPE][None, :]
    sign = jnp.where((lane % QK_ROPE) < HALF_ROPE, -1.0, 1.0).astype(F32)[None, :]
    return invf, sign


def _attn_kernel(*refs, n_riders):
    (q_ref, k_ref, v_ref), rider_in, (o_ref,), rider_out, (v_ext,) = _split_refs(refs, 3, 1, n_riders)
    _run_cast_riders(rider_in, rider_out)
    seq = q_ref.shape[0]
    tq = ATTN_BLOCK
    row = lax.broadcasted_iota(jnp.int32, (tq, tq), 0)
    col = lax.broadcasted_iota(jnp.int32, (tq, tq), 1)
    causal = col <= row
    for hd in range(ATTN_HEADS):
        v_ext[hd, :, 0:V_HEAD] = v_ref[:, hd * V_HEAD:(hd + 1) * V_HEAD]
        v_ext[hd, :, V_HEAD:2 * V_HEAD] = jnp.ones((seq, V_HEAD), BF16)

    def scores(hd, qi):
        r0 = qi * tq
        cols = slice(hd * HEAD_SLOT, (hd + 1) * HEAD_SLOT)
        q = q_ref[r0:r0 + tq, cols]
        sd = jnp.where(causal, _dot_nt(q, k_ref[r0:r0 + tq, cols]), -jnp.inf)
        sp = _dot_nt(q, k_ref[0:r0, cols]) if qi else None
        return sd, sp

    order = [(hd, qi) for hd in range(ATTN_HEADS) for qi in reversed(range(seq // tq))]
    ahead = [scores(*item) for item in order[:ATTN_AHEAD]]
    for n, (hd, qi) in enumerate(order):
        r0 = qi * tq
        r1 = r0 + tq
        sd, sp = ahead.pop(0)
        if n + ATTN_AHEAD < len(order):
            ahead.append(scores(*order[n + ATTN_AHEAD]))
        m = jnp.max(sd, axis=-1, keepdims=True)
        if qi:
            m = jnp.maximum(m, jnp.max(sp, axis=-1, keepdims=True))
        acc = _dot(jnp.exp2(sd - m).astype(BF16), v_ext[hd, r0:r1, :])
        if qi:
            acc = acc + _dot(jnp.exp2(sp - m).astype(BF16), v_ext[hd, 0:r0, :])
        o_ref[r0:r1, hd * V_HEAD:(hd + 1) * V_HEAD] = (
            acc[:, 0:V_HEAD] / acc[:, V_HEAD:2 * V_HEAD]).astype(BF16)


def _attention(q, k, v, batch, seq, riders):
    T = q.shape[0]
    groups = MLA_HEADS // ATTN_HEADS
    spec = lambda w: pl.BlockSpec((seq, ATTN_HEADS * w), lambda b, h: (b, h))
    r_in, r_out, r_shapes = _cast_rider_specs(riders, batch * groups, lambda b, h: b * groups + h)
    return pl.pallas_call(
        functools.partial(_attn_kernel, n_riders=len(riders)),
        grid=(batch, groups),
        in_specs=[spec(HEAD_SLOT), spec(HEAD_SLOT), spec(V_HEAD)] + r_in,
        out_specs=[spec(V_HEAD)] + r_out,
        out_shape=[jax.ShapeDtypeStruct((T, MLA_OUT), BF16)] + r_shapes,
        scratch_shapes=[pltpu.VMEM((ATTN_HEADS, seq, 2 * V_HEAD), BF16)],
        compiler_params=pltpu.CompilerParams(dimension_semantics=("parallel", "parallel"),
                                             vmem_limit_bytes=VMEM_LIMIT_BYTES),
        name="mla_attention",
    )(q, k, v, *[w for w, _ in riders])


def _out_mlp_kernel(x_ref, ma_ref, mb_ref, wo_ref, g_ref, w1_hbm, w2_hbm, fg_ref, *rest,
                    norm_a, final_norm):
    if norm_a:
        ag_ref, o_ref, h_buf, w1_buf, w2_buf, sem = rest
    else:
        o_ref, h_buf, w1_buf, w2_buf, sem = rest
    i = pl.program_id(0)
    half = ma_ref.shape[1]
    tf = w1_buf.shape[2]
    n_chunks = w1_hbm.shape[1] // tf
    assert n_chunks % MLP_SLOTS == 0, "chunk 0 must land in slot 0 of every row tile"

    def chunk_copies(k):
        slot = k % MLP_SLOTS
        return (pltpu.make_async_copy(w1_hbm.at[:, k * tf:(k + 1) * tf], w1_buf.at[slot],
                                      sem.at[0, slot]),
                pltpu.make_async_copy(w2_hbm.at[k * tf:(k + 1) * tf, :], w2_buf.at[slot],
                                      sem.at[1, slot]))

    @pl.when(i == 0)
    def _():
        for c in chunk_copies(0):
            c.start()

    for sb in range(x_ref.shape[0] // MLP_SUB):
        rows = slice(sb * MLP_SUB, (sb + 1) * MLP_SUB)
        ma = ma_ref[rows, :]
        if norm_a:
            ma = _rms(ma.astype(F32), ag_ref[...]).astype(BF16)
        x1 = (x_ref[rows, :] + _dot(ma, wo_ref[0:half, :])
              + _dot(mb_ref[rows, :], wo_ref[half:2 * half, :]))
        o_ref[rows, :] = x1
        h_buf[rows, :] = _rms(x1, g_ref[...]).astype(BF16)

    for k in range(n_chunks):
        slot = k % MLP_SLOTS
        for c in chunk_copies((k + 1) % n_chunks):
            c.start()
        for c in chunk_copies(k):
            c.wait()
        a = jnp.maximum(_dot(h_buf[...], w1_buf[slot]), 0.0)
        o_ref[...] += _dot((a * a).astype(BF16), w2_buf[slot])

    if final_norm:
        o_ref[...] = _rms(o_ref[...], fg_ref[...])

    @pl.when(i == pl.num_programs(0) - 1)
    def _():
        for c in chunk_copies(0):
            c.wait()


def _out_mlp(x, ma, mb, wo, g, w1, w2, fg, a_gain, *, final_norm):
    T = x.shape[0]
    tm, tf = MLP_ROWS, MLP_FF
    half = wo.shape[0] // 2
    norm_a = a_gain is not None
    in_specs = [pl.BlockSpec((tm, D_MODEL), lambda i: (i, 0)),
                pl.BlockSpec((tm, half), lambda i: (i, 0)),
                pl.BlockSpec((tm, half), lambda i: (i, mb.shape[1] // half - 1)),
                _const_spec(wo.shape), _const_spec(g.shape),
                pl.BlockSpec(memory_space=pl.ANY), pl.BlockSpec(memory_space=pl.ANY),
                _const_spec(fg.shape)]
    args = [x, ma, mb, wo, g, w1, w2, fg]
    if norm_a:
        in_specs.append(_const_spec(a_gain.shape))
        args.append(a_gain)
    return pl.pallas_call(
        functools.partial(_out_mlp_kernel, norm_a=norm_a, final_norm=final_norm),
        grid=(T // tm,),
        in_specs=in_specs,
        out_specs=pl.BlockSpec((tm, D_MODEL), lambda i: (i, 0)),
        out_shape=jax.ShapeDtypeStruct((T, D_MODEL), F32),
        scratch_shapes=[pltpu.VMEM((tm, D_MODEL), BF16),
                        pltpu.VMEM((MLP_SLOTS, D_MODEL, tf), BF16),
                        pltpu.VMEM((MLP_SLOTS, tf, D_MODEL), BF16),
                        pltpu.SemaphoreType.DMA((2, MLP_SLOTS))],
        compiler_params=pltpu.CompilerParams(dimension_semantics=("arbitrary",),
                                             vmem_limit_bytes=VMEM_LIMIT_BYTES),
        name="out_proj_mlp",
    )(*args)


def _conv_front_kernel(*refs, n_riders, tiles_per_seq):
    ins, rider_in, (o_ref,), rider_out, (hist,) = _split_refs(refs, 4, 1, n_riders)
    x_ref, g_ref, w_ref, cw_ref = ins
    tm = x_ref.shape[0]
    tn = CONV_COLS

    @pl.when(pl.program_id(0) == 0)
    def _():
        hist[...] = jnp.zeros_like(hist)

    seq_start = pl.program_id(0) % tiles_per_seq == 0
    h = _rms(x_ref[...], g_ref[...]).astype(BF16)
    row = lax.broadcasted_iota(jnp.int32, (tm, tn), 0)
    for cb in range(CONV_DIM // tn):
        c0 = cb * tn
        z = (_dot(h, w_ref[:, CONV_DIM + c0:CONV_DIM + c0 + tn])
             * _dot(h, w_ref[:, 2 * CONV_DIM + c0:2 * CONV_DIM + c0 + tn]))
        p1 = jnp.where(seq_start, 0.0, hist[7:8, c0:c0 + tn])
        p2 = jnp.where(seq_start, 0.0, hist[6:7, c0:c0 + tn])
        z1 = jnp.where(row == 0, p1, pltpu.roll(z, 1, 0))
        z2 = jnp.where(row == 0, p2, jnp.where(row == 1, p1, pltpu.roll(z, 2, 0)))
        conv = (cw_ref[0:1, c0:c0 + tn] * z2 + cw_ref[1:2, c0:c0 + tn] * z1
                + cw_ref[2:3, c0:c0 + tn] * z)
        o_ref[:, c0:c0 + tn] = (_dot(h, w_ref[:, c0:c0 + tn]) * conv).astype(BF16)
        hist[:, c0:c0 + tn] = z[tm - 8:tm, :]
    _run_cast_riders(rider_in, rider_out)


def _conv_front(x, g, w_in, conv_w, layer, seq, riders):
    T = x.shape[0]
    tm = CONV_ROWS
    steps = T // tm
    r_in, r_out, r_shapes = _cast_rider_specs(riders, steps, lambda i: i)
    return pl.pallas_call(
        functools.partial(_conv_front_kernel, n_riders=len(riders), tiles_per_seq=seq // tm),
        grid=(steps,),
        in_specs=[pl.BlockSpec((tm, D_MODEL), lambda i: (i, 0)), _const_spec(g.shape),
                  _const_spec(w_in.shape), _layer_spec(conv_w, layer)] + r_in,
        out_specs=[pl.BlockSpec((tm, CONV_DIM), lambda i: (i, 0))] + r_out,
        out_shape=[jax.ShapeDtypeStruct((T, CONV_DIM), BF16)] + r_shapes,
        scratch_shapes=[pltpu.VMEM((8, CONV_DIM), F32)],
        compiler_params=pltpu.CompilerParams(dimension_semantics=("arbitrary",),
                                             vmem_limit_bytes=VMEM_LIMIT_BYTES),
        name="conv_front",
    )(x, g, w_in, conv_w, *[w for w, _ in riders])


def _even_weights(w_in, w_uq, w_ukv):
    c1 = Q_LORA
    c2 = c1 + KV_LORA
    c3 = c2 + QK_ROPE
    k1 = w_in[:, c2:c2 + HALF_ROPE]
    k2 = w_in[:, c2 + HALF_ROPE:c3]
    wlat = w_in[:, :c2]
    wuv = w_in[:, c3:]
    wkr = jnp.concatenate([k1, k2, k1, k2, k2, k1, k2, k1], axis=1)
    wq = w_uq.reshape(Q_LORA, MLA_HEADS, QK_NOPE + QK_ROPE)
    r1 = wq[:, :, QK_NOPE:QK_NOPE + HALF_ROPE]
    r2 = wq[:, :, QK_NOPE + HALF_ROPE:]
    pairs = LANES // QK_ROPE

    def rope_cols(a, b):
        w = jnp.concatenate([a, b], axis=-1).reshape(Q_LORA, pairs, MLA_HEADS // pairs, QK_ROPE)
        return w.transpose(0, 2, 1, 3).reshape(Q_LORA, -1)

    wuq = jnp.concatenate([wq[:, :, :QK_NOPE].reshape(Q_LORA, -1), rope_cols(r1, r2),
                           rope_cols(r2, r1)], axis=1)
    wkv = w_ukv.reshape(KV_LORA, MLA_HEADS, QK_NOPE + V_HEAD)
    wukv = jnp.concatenate([wkv[:, :, :QK_NOPE].reshape(KV_LORA, -1),
                            wkv[:, :, QK_NOPE:].reshape(KV_LORA, -1)], axis=1)
    return tuple(w.astype(BF16) for w in (wlat, wuv, wkr, wuq, wukv))


def kernel(x, positions, e_norm_mix, e_w_in, e_q_norm, e_w_uq, e_kv_norm, e_w_ukv, e_v_norm,
           e_sgu_w, e_sgu_b, e_mla_out_norm, e_sgu_out_norm, e_w_out, o_norm_mix, o_w_in,
           o_conv_w, o_w_out, mlp_norm, mlp_w1, mlp_w2, final_norm):
    batch, seq, d = x.shape
    T = batch * seq
    depth = mlp_norm.shape[0]
    xf = x.reshape(T, d)
    pos = positions.reshape(T, 1)
    invf, sign = _rope_rows()
    fg = final_norm[None, :]
    row2d = lambda a: a.reshape(1, -1)

    o_wi = None
    for layer in range(depth):
        i = layer // 2
        mlp_riders = [(mlp_w1, layer), (mlp_w2, layer)]
        if layer % 2 == 0:
            wlat, wuv, wkr, wuq, wukv = _even_weights(e_w_in[i], e_w_uq[i], e_w_ukv[i])
            sgub = jnp.repeat(e_sgu_b[i].T, SGU_CH, axis=1)
            q, k, v, sn, wo = _even_front(
                xf, pos, invf, sign, row2d(e_norm_mix[i]), wlat, wuv, wkr, row2d(e_q_norm[i]), wuq,
                row2d(e_kv_norm[i]), wukv, row2d(e_v_norm[i]), e_sgu_w[i], sgub,
                row2d(e_sgu_out_norm[i]), riders=[(e_w_out, i)])
            next_in = [(o_w_in, (layer + 1) // 2)] if layer + 1 < depth else []
            ma, w1, w2, *rest = _attention(q, k, v, batch, seq, mlp_riders + next_in)
            o_wi = rest[0] if rest else None
            mb, a_gain = sn, row2d(e_mla_out_norm[i])
        else:
            ma, wo, w1, w2 = _conv_front(xf, row2d(o_norm_mix[i]), o_wi, o_conv_w, i, seq,
                                         riders=[(o_w_out, i)] + mlp_riders)
            mb, a_gain = ma, None
        xf = _out_mlp(xf, ma, mb, wo, row2d(mlp_norm[layer]), w1, w2, fg, a_gain,
                      final_norm=(layer == depth - 1))
    return xf.reshape(batch, seq, d)
```

```python
import functools
import math

import jax
import jax.numpy as jnp
from jax import lax
from jax.experimental import pallas as pl
from jax.experimental.pallas import tpu as pltpu

D_MODEL = 2048
MLA_HEADS = 8
Q_LORA = 512
KV_LORA = 512
QK_NOPE = 128
QK_ROPE = 64
V_HEAD = 128
ROPE_BASE = 10000.0
SGU_GROUPS = 8
SGU_CH = 128
CHUNK = 128
CONV_DIM = D_MODEL
CONV_WIDTH = 3
D_FF = 4 * D_MODEL
EPS = 1e-6
MLA_OUT = MLA_HEADS * V_HEAD
SGU_OUT = SGU_GROUPS * SGU_CH
HALF_ROPE = QK_ROPE // 2
LOG2_E = math.log2(math.e)
HEAD_SLOT = 2 * QK_NOPE

LANES = 128
BF16_ROWS = 16
VMEM_LIMIT_BYTES = 60 * 1024 * 1024

FRONT_ROWS = 512
FRONT_SUB = 256
ATTN_BLOCK = 256
ATTN_HEADS = 2
ATTN_AHEAD = 2
MLP_ROWS = 512
MLP_SUB = 256
MLP_FF = 1024
MLP_SLOTS = 2
CONV_ROWS = 512
CONV_COLS = 1024

BF16 = jnp.bfloat16
F32 = jnp.float32


def _dot(a, b):
    return jnp.dot(a, b, preferred_element_type=F32)


def _dot_nt(a, b):
    return lax.dot_general(a, b, (((1,), (1,)), ((), ())), preferred_element_type=F32)


def _rms(x, g):
    return x * lax.rsqrt(jnp.mean(x * x, axis=-1, keepdims=True) + EPS) * g


def _const_spec(shape):
    return pl.BlockSpec(shape, lambda *_: (0,) * len(shape), pipeline_mode=pl.Buffered(1))


def _layer_spec(w, layer):
    return pl.BlockSpec((None,) + w.shape[1:], lambda *_: (layer, 0, 0), pipeline_mode=pl.Buffered(1))


def _cast_rider_specs(riders, steps, step_of):
    ins, outs, shapes = [], [], []
    for w, layer in riders:
        _, rows, cols = w.shape
        rb = rows // steps
        assert rb * steps == rows and rb % BF16_ROWS == 0, (w.shape, steps)
        ins.append(pl.BlockSpec((None, rb, cols), lambda *g, layer=layer: (layer, step_of(*g), 0)))
        outs.append(pl.BlockSpec((rb, cols), lambda *g: (step_of(*g), 0)))
        shapes.append(jax.ShapeDtypeStruct((rows, cols), BF16))
    return ins, outs, shapes


def _split_refs(refs, n_in, n_out, n_riders):
    cuts = [n_in, n_in + n_riders, n_in + n_riders + n_out, n_in + 2 * n_riders + n_out]
    return (refs[:cuts[0]], refs[cuts[0]:cuts[1]], refs[cuts[1]:cuts[2]], refs[cuts[2]:cuts[3]],
            refs[cuts[3]:])


def _run_cast_riders(srcs, dsts):
    for src, dst in zip(srcs, dsts):
        dst[...] = src[...].astype(BF16)


def _rope_tables(pos_ref, r0, invf, lane_group):
    n = LANES // HALF_ROPE
    seg = lane_group.shape[0]
    packed = jnp.zeros((seg, LANES), F32)
    for j in range(n):
        p = pos_ref[r0 + j * seg:r0 + (j + 1) * seg, :].astype(F32)
        packed = jnp.where(lane_group == j, p, packed)
    ang = packed * invf
    tables = []
    for t in (jnp.cos(ang), jnp.sin(ang)):
        parts = []
        for j in range(n):
            one = jnp.where(lane_group == j, t, 0.0)
            full = one
            for s in range(1, n):
                full = full + pltpu.roll(one, s * HALF_ROPE, 1)
            parts.append(full)
        tables.append(jnp.concatenate(parts, axis=0))
    return tables


def _even_front_kernel(*refs, n_riders):
    ins, rider_in, outs, rider_out, (vn_buf, s_buf, sw_buf) = _split_refs(refs, 16, 4, n_riders)
    (x_ref, pos_ref, invf_ref, sign_ref, g_ref, wlat_ref, wuv_ref, wkr_ref, qg_ref, wuq_ref,
     kvg_ref, wukv_ref, vng_ref, sguw_ref, sgub_ref, sgn_ref) = ins
    q_ref, k_ref, v_ref, s_ref = outs

    @pl.when(pl.program_id(0) == 0)
    def _():
        row = lax.broadcasted_iota(jnp.int32, (CHUNK, CHUNK), 0)
        col = lax.broadcasted_iota(jnp.int32, (CHUNK, CHUNK), 1)
        for g in range(SGU_GROUPS):
            sw_buf[g] = jnp.where(col <= row, sguw_ref[g], 0.0).astype(BF16)

    scale = (QK_NOPE + QK_ROPE) ** -0.5 * LOG2_E
    n_rope_cols = MLA_HEADS * QK_ROPE // LANES
    lane = lax.broadcasted_iota(jnp.int32, (FRONT_SUB, LANES), 1)
    low_lanes = lane < QK_ROPE
    first_half = lane % QK_ROPE < HALF_ROPE
    seg = FRONT_SUB // (LANES // HALF_ROPE)
    lane_group = lax.broadcasted_iota(jnp.int32, (seg, LANES), 1) // HALF_ROPE

    def normed(sb):
        return _rms(x_ref[sb * FRONT_SUB:(sb + 1) * FRONT_SUB, :], g_ref[...]).astype(BF16)

    def first_dots(h):
        return dict(
            cq=_dot(h, wlat_ref[:, 0:Q_LORA]),
            ckv=_dot(h, wlat_ref[:, Q_LORA:Q_LORA + KV_LORA]),
            ka=_dot(h, wkr_ref[:, 0:LANES]),
            kb=_dot(h, wkr_ref[:, LANES:2 * LANES]),
            u=_dot(h, wuv_ref[:, 0:SGU_OUT]),
            v=_dot(h, wuv_ref[:, SGU_OUT:2 * SGU_OUT]))

    def mla_part(sb, raw):
        rows = slice(sb * FRONT_SUB, (sb + 1) * FRONT_SUB)
        cos, sin = _rope_tables(pos_ref, sb * FRONT_SUB, invf_ref[...], lane_group)
        sin = sin * sign_ref[...]

        cq = _rms(raw["cq"], qg_ref[...]).astype(BF16)
        q = _dot(cq, wuq_ref[...])
        cos_q = jnp.concatenate([cos] * n_rope_cols, axis=1)
        sin_q = jnp.concatenate([sin] * n_rope_cols, axis=1)
        qn = (q[:, :MLA_OUT] * scale).astype(BF16)
        qa = q[:, MLA_OUT:]
        qb = jnp.concatenate(
            [jnp.where(first_half, pltpu.roll(qa[:, c * LANES:(c + 1) * LANES], LANES - HALF_ROPE, 1),
                       pltpu.roll(qa[:, c * LANES:(c + 1) * LANES], HALF_ROPE, 1))
             for c in range(n_rope_cols)], axis=1)
        qr = ((qa * cos_q + qb * sin_q) * scale).astype(BF16)

        ckv = _rms(raw["ckv"], kvg_ref[...]).astype(BF16)
        kv = _dot(ckv, wukv_ref[...])
        kn = kv[:, :MLA_OUT].astype(BF16)
        v_ref[rows, :] = kv[:, MLA_OUT:].astype(BF16)

        kr = (raw["ka"] * cos + raw["kb"] * sin).astype(BF16)
        for hd in range(MLA_HEADS):
            lo = hd * HEAD_SLOT
            src = slice(hd * QK_NOPE, (hd + 1) * QK_NOPE)
            pair = hd % n_rope_cols
            rope = qr[:, pair * LANES:(pair + 1) * LANES]
            keep = low_lanes if hd < n_rope_cols else ~low_lanes
            q_ref[rows, lo:lo + QK_NOPE] = qn[:, src]
            q_ref[rows, lo + QK_NOPE:lo + HEAD_SLOT] = jnp.where(keep, rope, jnp.zeros_like(rope))
            k_ref[rows, lo:lo + QK_NOPE] = kn[:, src]
            k_ref[rows, lo + QK_NOPE:lo + HEAD_SLOT] = kr

    def sgu_norm(sb, raw):
        rows = slice(sb * FRONT_SUB, (sb + 1) * FRONT_SUB)
        u = jax.nn.gelu(raw["u"])
        vv = jax.nn.gelu(raw["v"])
        for g in range(SGU_GROUPS):
            lo = g * SGU_CH
            vg = vv[:, lo:lo + SGU_CH]
            mu = jnp.mean(vg, axis=-1, keepdims=True)
            d = vg - mu
            var = jnp.mean(d * d, axis=-1, keepdims=True)
            vn_buf[rows, lo:lo + SGU_CH] = (d * lax.rsqrt(var + EPS)
                                            * vng_ref[:, lo:lo + SGU_CH]).astype(BF16)
        return u

    def sgu_mix(sb, u):
        rows = slice(sb * FRONT_SUB, (sb + 1) * FRONT_SUB)
        for g in range(SGU_GROUPS):
            lo = g * SGU_CH
            for c in range(FRONT_SUB // CHUNK):
                t0 = sb * FRONT_SUB + c * CHUNK
                y = _dot(sw_buf[g], vn_buf[t0:t0 + CHUNK, lo:lo + SGU_CH])
                s_buf[t0:t0 + CHUNK, lo:lo + SGU_CH] = y + sgub_ref[:, lo:lo + SGU_CH]
        s_ref[rows, :] = _rms(u * s_buf[rows, :], sgn_ref[...]).astype(BF16)

    subs = list(range(x_ref.shape[0] // FRONT_SUB))
    raw = first_dots(normed(subs[0]))
    for sb in subs:
        nxt = first_dots(normed(sb + 1)) if sb + 1 < len(subs) else None
        mla_part(sb, raw)
        sgu_mix(sb, sgu_norm(sb, raw))
        raw = nxt
    _run_cast_riders(rider_in, rider_out)


def _even_front(x, pos, invf, sign, g, wlat, wuv, wkr, qg, wuq, kvg, wukv, vng, sguw, sgub, sgn,
                riders):
    T = x.shape[0]
    tm = FRONT_ROWS
    steps = T // tm
    row = lambda w: pl.BlockSpec((tm, w), lambda i: (i, 0))
    out_widths = (MLA_HEADS * HEAD_SLOT, MLA_HEADS * HEAD_SLOT, MLA_OUT, SGU_OUT)
    r_in, r_out, r_shapes = _cast_rider_specs(riders, steps, lambda i: i)
    consts = (invf, sign, g, wlat, wuv, wkr, qg, wuq, kvg, wukv, vng, sguw, sgub, sgn)
    return pl.pallas_call(
        functools.partial(_even_front_kernel, n_riders=len(riders)),
        grid=(steps,),
        in_specs=[row(D_MODEL), row(1)] + [_const_spec(c.shape) for c in consts] + r_in,
        out_specs=[row(w) for w in out_widths] + r_out,
        out_shape=[jax.ShapeDtypeStruct((T, w), BF16) for w in out_widths] + r_shapes,
        scratch_shapes=[pltpu.VMEM((tm, SGU_OUT), BF16), pltpu.VMEM((tm, SGU_OUT), F32),
                        pltpu.VMEM((SGU_GROUPS, CHUNK, CHUNK), BF16)],
        compiler_params=pltpu.CompilerParams(dimension_semantics=("arbitrary",),
                                             vmem_limit_bytes=VMEM_LIMIT_BYTES),
        name="even_front",
    )(x, pos, *consts, *[w for w, _ in riders])


def _rope_rows():
    lane = jnp.arange(LANES)
    inv_freq = ROPE_BASE ** (-jnp.arange(0, QK_ROPE, 2, dtype=F32) / QK_ROPE)
    invf = inv_freq[lane % HALF_ROPE][None, :]
    sign = jnp.where((lane % QK_ROPE) < HALF_ROPE, -1.0, 1.0).astype(F32)[None, :]
    return invf, sign


def _attn_kernel(*refs, n_riders):
    (q_ref, k_ref, v_ref), rider_in, (o_ref,), rider_out, (v_ext,) = _split_refs(refs, 3, 1, n_riders)
    _run_cast_riders(rider_in, rider_out)
    seq = q_ref.shape[0]
    tq = ATTN_BLOCK
    row = lax.broadcasted_iota(jnp.int32, (tq, tq), 0)
    col = lax.broadcasted_iota(jnp.int32, (tq, tq), 1)
    causal = col <= row
    for hd in range(ATTN_HEADS):
        v_ext[hd, :, 0:V_HEAD] = v_ref[:, hd * V_HEAD:(hd + 1) * V_HEAD]
        v_ext[hd, :, V_HEAD:2 * V_HEAD] = jnp.ones((seq, V_HEAD), BF16)

    def scores(hd, qi):
        r0 = qi * tq
        cols = slice(hd * HEAD_SLOT, (hd + 1) * HEAD_SLOT)
        q = q_ref[r0:r0 + tq, cols]
        sd = jnp.where(causal, _dot_nt(q, k_ref[r0:r0 + tq, cols]), -jnp.inf)
        sp = _dot_nt(q, k_ref[0:r0, cols]) if qi else None
        return sd, sp

    order = [(hd, qi) for hd in range(ATTN_HEADS) for qi in reversed(range(seq // tq))]
    ahead = [scores(*item) for item in order[:ATTN_AHEAD]]
    for n, (hd, qi) in enumerate(order):
        r0 = qi * tq
        r1 = r0 + tq
        sd, sp = ahead.pop(0)
        if n + ATTN_AHEAD < len(order):
            ahead.append(scores(*order[n + ATTN_AHEAD]))
        m = jnp.max(sd, axis=-1, keepdims=True)
        if qi:
            m = jnp.maximum(m, jnp.max(sp, axis=-1, keepdims=True))
        acc = _dot(jnp.exp2(sd - m).astype(BF16), v_ext[hd, r0:r1, :])
        if qi:
            acc = acc + _dot(jnp.exp2(sp - m).astype(BF16), v_ext[hd, 0:r0, :])
        o_ref[r0:r1, hd * V_HEAD:(hd + 1) * V_HEAD] = (
            acc[:, 0:V_HEAD] / acc[:, V_HEAD:2 * V_HEAD]).astype(BF16)


def _attention(q, k, v, batch, seq, riders):
    T = q.shape[0]
    groups = MLA_HEADS // ATTN_HEADS
    spec = lambda w: pl.BlockSpec((seq, ATTN_HEADS * w), lambda b, h: (b, h))
    r_in, r_out, r_shapes = _cast_rider_specs(riders, batch * groups, lambda b, h: b * groups + h)
    return pl.pallas_call(
        functools.partial(_attn_kernel, n_riders=len(riders)),
        grid=(batch, groups),
        in_specs=[spec(HEAD_SLOT), spec(HEAD_SLOT), spec(V_HEAD)] + r_in,
        out_specs=[spec(V_HEAD)] + r_out,
        out_shape=[jax.ShapeDtypeStruct((T, MLA_OUT), BF16)] + r_shapes,
        scratch_shapes=[pltpu.VMEM((ATTN_HEADS, seq, 2 * V_HEAD), BF16)],
        compiler_params=pltpu.CompilerParams(dimension_semantics=("parallel", "parallel"),
                                             vmem_limit_bytes=VMEM_LIMIT_BYTES),
        name="mla_attention",
    )(q, k, v, *[w for w, _ in riders])


def _out_mlp_kernel(x_ref, ma_ref, mb_ref, wo_ref, g_ref, w1_hbm, w2_hbm, fg_ref, *rest,
                    norm_a, final_norm):
    if norm_a:
        ag_ref, o_ref, h_buf, w1_buf, w2_buf, sem = rest
    else:
        o_ref, h_buf, w1_buf, w2_buf, sem = rest
    i = pl.program_id(0)
    half = ma_ref.shape[1]
    tf = w1_buf.shape[2]
    n_chunks = w1_hbm.shape[1] // tf
    assert n_chunks % MLP_SLOTS == 0, "chunk 0 must land in slot 0 of every row tile"

    def chunk_copies(k):
        slot = k % MLP_SLOTS
        return (pltpu.make_async_copy(w1_hbm.at[:, k * tf:(k + 1) * tf], w1_buf.at[slot],
                                      sem.at[0, slot]),
                pltpu.make_async_copy(w2_hbm.at[k * tf:(k + 1) * tf, :], w2_buf.at[slot],
                                      sem.at[1, slot]))

    @pl.when(i == 0)
    def _():
        for c in chunk_copies(0):
            c.start()

    for sb in range(x_ref.shape[0] // MLP_SUB):
        rows = slice(sb * MLP_SUB, (sb + 1) * MLP_SUB)
        ma = ma_ref[rows, :]
        if norm_a:
            ma = _rms(ma.astype(F32), ag_ref[...]).astype(BF16)
        x1 = (x_ref[rows, :] + _dot(ma, wo_ref[0:half, :])
              + _dot(mb_ref[rows, :], wo_ref[half:2 * half, :]))
        o_ref[rows, :] = x1
        h_buf[rows, :] = _rms(x1, g_ref[...]).astype(BF16)

    for k in range(n_chunks):
        slot = k % MLP_SLOTS
        for c in chunk_copies((k + 1) % n_chunks):
            c.start()
        for c in chunk_copies(k):
            c.wait()
        a = jnp.maximum(_dot(h_buf[...], w1_buf[slot]), 0.0)
        o_ref[...] += _dot((a * a).astype(BF16), w2_buf[slot])

    if final_norm:
        o_ref[...] = _rms(o_ref[...], fg_ref[...])

    @pl.when(i == pl.num_programs(0) - 1)
    def _():
        for c in chunk_copies(0):
            c.wait()


def _out_mlp(x, ma, mb, wo, g, w1, w2, fg, a_gain, *, final_norm):
    T = x.shape[0]
    tm, tf = MLP_ROWS, MLP_FF
    half = wo.shape[0] // 2
    norm_a = a_gain is not None
    in_specs = [pl.BlockSpec((tm, D_MODEL), lambda i: (i, 0)),
                pl.BlockSpec((tm, half), lambda i: (i, 0)),
                pl.BlockSpec((tm, half), lambda i: (i, mb.shape[1] // half - 1)),
                _const_spec(wo.shape), _const_spec(g.shape),
                pl.BlockSpec(memory_space=pl.ANY), pl.BlockSpec(memory_space=pl.ANY),
                _const_spec(fg.shape)]
    args = [x, ma, mb, wo, g, w1, w2, fg]
    if norm_a:
        in_specs.append(_const_spec(a_gain.shape))
        args.append(a_gain)
    return pl.pallas_call(
        functools.partial(_out_mlp_kernel, norm_a=norm_a, final_norm=final_norm),
        grid=(T // tm,),
        in_specs=in_specs,
        out_specs=pl.BlockSpec((tm, D_MODEL), lambda i: (i, 0)),
        out_shape=jax.ShapeDtypeStruct((T, D_MODEL), F32),
        scratch_shapes=[pltpu.VMEM((tm, D_MODEL), BF16),
                        pltpu.VMEM((MLP_SLOTS, D_MODEL, tf), BF16),
                        pltpu.VMEM((MLP_SLOTS, tf, D_MODEL), BF16),
                        pltpu.SemaphoreType.DMA((2, MLP_SLOTS))],
        compiler_params=pltpu.CompilerParams(dimension_semantics=("arbitrary",),
                                             vmem_limit_bytes=VMEM_LIMIT_BYTES),
        name="out_proj_mlp",
    )(*args)


def _conv_front_kernel(*refs, n_riders, tiles_per_seq):
    ins, rider_in, (o_ref,), rider_out, (hist,) = _split_refs(refs, 4, 1, n_riders)
    x_ref, g_ref, w_ref, cw_ref = ins
    tm = x_ref.shape[0]
    tn = CONV_COLS

    @pl.when(pl.program_id(0) == 0)
    def _():
        hist[...] = jnp.zeros_like(hist)

    seq_start = pl.program_id(0) % tiles_per_seq == 0
    h = _rms(x_ref[...], g_ref[...]).astype(BF16)
    row = lax.broadcasted_iota(jnp.int32, (tm, tn), 0)
    for cb in range(CONV_DIM // tn):
        c0 = cb * tn
        z = (_dot(h, w_ref[:, CONV_DIM + c0:CONV_DIM + c0 + tn])
             * _dot(h, w_ref[:, 2 * CONV_DIM + c0:2 * CONV_DIM + c0 + tn]))
        p1 = jnp.where(seq_start, 0.0, hist[7:8, c0:c0 + tn])
        p2 = jnp.where(seq_start, 0.0, hist[6:7, c0:c0 + tn])
        z1 = jnp.where(row == 0, p1, pltpu.roll(z, 1, 0))
        z2 = jnp.where(row == 0, p2, jnp.where(row == 1, p1, pltpu.roll(z, 2, 0)))
        conv = (cw_ref[0:1, c0:c0 + tn] * z2 + cw_ref[1:2, c0:c0 + tn] * z1
                + cw_ref[2:3, c0:c0 + tn] * z)
        o_ref[:, c0:c0 + tn] = (_dot(h, w_ref[:, c0:c0 + tn]) * conv).astype(BF16)
        hist[:, c0:c0 + tn] = z[tm - 8:tm, :]
    _run_cast_riders(rider_in, rider_out)


def _conv_front(x, g, w_in, conv_w, layer, seq, riders):
    T = x.shape[0]
    tm = CONV_ROWS
    steps = T // tm
    r_in, r_out, r_shapes = _cast_rider_specs(riders, steps, lambda i: i)
    return pl.pallas_call(
        functools.partial(_conv_front_kernel, n_riders=len(riders), tiles_per_seq=seq // tm),
        grid=(steps,),
        in_specs=[pl.BlockSpec((tm, D_MODEL), lambda i: (i, 0)), _const_spec(g.shape),
                  _const_spec(w_in.shape), _layer_spec(conv_w, layer)] + r_in,
        out_specs=[pl.BlockSpec((tm, CONV_DIM), lambda i: (i, 0))] + r_out,
        out_shape=[jax.ShapeDtypeStruct((T, CONV_DIM), BF16)] + r_shapes,
        scratch_shapes=[pltpu.VMEM((8, CONV_DIM), F32)],
        compiler_params=pltpu.CompilerParams(dimension_semantics=("arbitrary",),
                                             vmem_limit_bytes=VMEM_LIMIT_BYTES),
        name="conv_front",
    )(x, g, w_in, conv_w, *[w for w, _ in riders])


def _even_weights(w_in, w_uq, w_ukv):
    c1 = Q_LORA
    c2 = c1 + KV_LORA
    c3 = c2 + QK_ROPE
    k1 = w_in[:, c2:c2 + HALF_ROPE]
    k2 = w_in[:, c2 + HALF_ROPE:c3]
    wlat = w_in[:, :c2]
    wuv = w_in[:, c3:]
    wkr = jnp.concatenate([k1, k2, k1, k2, k2, k1, k2, k1], axis=1)
    wq = w_uq.reshape(Q_LORA, MLA_HEADS, QK_NOPE + QK_ROPE)
    r1 = wq[:, :, QK_NOPE:QK_NOPE + HALF_ROPE]
    r2 = wq[:, :, QK_NOPE + HALF_ROPE:]
    pairs = LANES // QK_ROPE

    def rope_cols(a, b):
        w = jnp.concatenate([a, b], axis=-1).reshape(Q_LORA, pairs, MLA_HEADS // pairs, QK_ROPE)
        return w.transpose(0, 2, 1, 3).reshape(Q_LORA, -1)

    wuq = jnp.concatenate([wq[:, :, :QK_NOPE].reshape(Q_LORA, -1), rope_cols(r1, r2)], axis=1)
    wkv = w_ukv.reshape(KV_LORA, MLA_HEADS, QK_NOPE + V_HEAD)
    wukv = jnp.concatenate([wkv[:, :, :QK_NOPE].reshape(KV_LORA, -1),
                            wkv[:, :, QK_NOPE:].reshape(KV_LORA, -1)], axis=1)
    return tuple(w.astype(BF16) for w in (wlat, wuv, wkr, wuq, wukv))


def kernel(x, positions, e_norm_mix, e_w_in, e_q_norm, e_w_uq, e_kv_norm, e_w_ukv, e_v_norm,
           e_sgu_w, e_sgu_b, e_mla_out_norm, e_sgu_out_norm, e_w_out, o_norm_mix, o_w_in,
           o_conv_w, o_w_out, mlp_norm, mlp_w1, mlp_w2, final_norm):
    batch, seq, d = x.shape
    T = batch * seq
    depth = mlp_norm.shape[0]
    xf = x.reshape(T, d)
    pos = positions.reshape(T, 1)
    invf, sign = _rope_rows()
    fg = final_norm[None, :]
    row2d = lambda a: a.reshape(1, -1)

    o_wi = None
    for layer in range(depth):
        i = layer // 2
        mlp_riders = [(mlp_w1, layer), (mlp_w2, layer)]
        if layer % 2 == 0:
            wlat, wuv, wkr, wuq, wukv = _even_weights(e_w_in[i], e_w_uq[i], e_w_ukv[i])
            sgub = jnp.repeat(e_sgu_b[i].T, SGU_CH, axis=1)
            q, k, v, sn, wo = _even_front(
                xf, pos, invf, sign, row2d(e_norm_mix[i]), wlat, wuv, wkr, row2d(e_q_norm[i]), wuq,
                row2d(e_kv_norm[i]), wukv, row2d(e_v_norm[i]), e_sgu_w[i], sgub,
                row2d(e_sgu_out_norm[i]), riders=[(e_w_out, i)])
            next_in = [(o_w_in, (layer + 1) // 2)] if layer + 1 < depth else []
            ma, w1, w2, *rest = _attention(q, k, v, batch, seq, mlp_riders + next_in)
            o_wi = rest[0] if rest else None
            mb, a_gain = sn, row2d(e_mla_out_norm[i])
        else:
            ma, wo, w1, w2 = _conv_front(xf, row2d(o_norm_mix[i]), o_wi, o_conv_w, i, seq,
                                         riders=[(o_w_out, i)] + mlp_riders)
            mb, a_gain = ma, None
        xf = _out_mlp(xf, ma, mb, wo, row2d(mlp_norm[layer]), w1, w2, fg, a_gain,
                      final_norm=(layer == depth - 1))
    return xf.reshape(batch, seq, d)
```

```python
import functools
import math

import jax
import jax.numpy as jnp
from jax import lax
from jax.experimental import pallas as pl
from jax.experimental.pallas import tpu as pltpu

D_MODEL = 2048
MLA_HEADS = 8
Q_LORA = 512
KV_LORA = 512
QK_NOPE = 128
QK_ROPE = 64
V_HEAD = 128
ROPE_BASE = 10000.0
SGU_GROUPS = 8
SGU_CH = 128
CHUNK = 128
CONV_DIM = D_MODEL
CONV_WIDTH = 3
D_FF = 4 * D_MODEL
EPS = 1e-6
MLA_OUT = MLA_HEADS * V_HEAD
SGU_OUT = SGU_GROUPS * SGU_CH
HALF_ROPE = QK_ROPE // 2
LOG2_E = math.log2(math.e)
HEAD_SLOT = 2 * QK_NOPE

LANES = 128
F32_ROWS = 8
BF16_ROWS = 16
VMEM_LIMIT_BYTES = 60 * 1024 * 1024

FRONT_ROWS = 512
FRONT_SUB = 256
ATTN_BLOCK = 256
ATTN_HEADS = 2
ATTN_AHEAD = 2
MLP_ROWS = 512
MLP_SUB = 256
MLP_FF = 1024
MLP_SLOTS = 2
CONV_ROWS = 512
CONV_COLS = 1024

BF16 = jnp.bfloat16
F32 = jnp.float32


def _dot(a, b):
    return jnp.dot(a, b, preferred_element_type=F32)


def _dot_nt(a, b):
    return lax.dot_general(a, b, (((1,), (1,)), ((), ())), preferred_element_type=F32)


def _rms(x, g):
    return x * lax.rsqrt(jnp.mean(x * x, axis=-1, keepdims=True) + EPS) * g


def _const_spec(shape):
    return pl.BlockSpec(shape, lambda *_: (0,) * len(shape), pipeline_mode=pl.Buffered(1))


def _layer_spec(w, layer):
    return pl.BlockSpec((None,) + w.shape[1:], lambda *_: (layer, 0, 0), pipeline_mode=pl.Buffered(1))


def _cast_rider_specs(riders, steps, step_of):
    ins, outs, shapes = [], [], []
    for w, layer in riders:
        _, rows, cols = w.shape
        rb = rows // steps
        assert rb * steps == rows and rb % BF16_ROWS == 0, (w.shape, steps)
        ins.append(pl.BlockSpec((None, rb, cols), lambda *g, layer=layer: (layer, step_of(*g), 0)))
        outs.append(pl.BlockSpec((rb, cols), lambda *g: (step_of(*g), 0)))
        shapes.append(jax.ShapeDtypeStruct((rows, cols), BF16))
    return ins, outs, shapes


def _split_refs(refs, n_in, n_out, n_riders):
    cuts = [n_in, n_in + n_riders, n_in + n_riders + n_out, n_in + 2 * n_riders + n_out]
    return (refs[:cuts[0]], refs[cuts[0]:cuts[1]], refs[cuts[1]:cuts[2]], refs[cuts[2]:cuts[3]],
            refs[cuts[3]:])


def _run_cast_riders(srcs, dsts):
    for src, dst in zip(srcs, dsts):
        dst[...] = src[...].astype(BF16)


def _rope_tables(pos_ref, r0, invf, lane_group):
    n = LANES // HALF_ROPE
    seg = lane_group.shape[0]
    packed = jnp.zeros((seg, LANES), F32)
    for j in range(n):
        p = pos_ref[r0 + j * seg:r0 + (j + 1) * seg, :].astype(F32)
        packed = jnp.where(lane_group == j, p, packed)
    ang = packed * invf
    tables = []
    for t in (jnp.cos(ang), jnp.sin(ang)):
        parts = []
        for j in range(n):
            one = jnp.where(lane_group == j, t, 0.0)
            full = one
            for s in range(1, n):
                full = full + pltpu.roll(one, s * HALF_ROPE, 1)
            parts.append(full)
        tables.append(jnp.concatenate(parts, axis=0))
    return tables


def _even_front_kernel(*refs, n_riders):
    ins, rider_in, outs, rider_out, (vn_buf, s_buf, sw_buf) = _split_refs(refs, 16, 4, n_riders)
    (x_ref, pos_ref, invf_ref, sign_ref, g_ref, wlat_ref, wuv_ref, wkr_ref, qg_ref, wuq_ref,
     kvg_ref, wukv_ref, vng_ref, sguw_ref, sgub_ref, sgn_ref) = ins
    q_ref, k_ref, v_ref, s_ref = outs

    @pl.when(pl.program_id(0) == 0)
    def _():
        row = lax.broadcasted_iota(jnp.int32, (CHUNK, CHUNK), 0)
        col = lax.broadcasted_iota(jnp.int32, (CHUNK, CHUNK), 1)
        for g in range(SGU_GROUPS):
            sw_buf[g] = jnp.where(col <= row, sguw_ref[g], 0.0).astype(BF16)

    scale = (QK_NOPE + QK_ROPE) ** -0.5 * LOG2_E
    n_rope_cols = MLA_HEADS * QK_ROPE // LANES
    lane = lax.broadcasted_iota(jnp.int32, (FRONT_SUB, LANES), 1)
    low_lanes = lane < QK_ROPE
    first_half = lane % QK_ROPE < HALF_ROPE
    seg = FRONT_SUB // (LANES // HALF_ROPE)
    lane_group = lax.broadcasted_iota(jnp.int32, (seg, LANES), 1) // HALF_ROPE

    def normed(sb):
        return _rms(x_ref[sb * FRONT_SUB:(sb + 1) * FRONT_SUB, :], g_ref[...]).astype(BF16)

    def first_dots(h):
        return dict(
            cq=_dot(h, wlat_ref[:, 0:Q_LORA]),
            ckv=_dot(h, wlat_ref[:, Q_LORA:Q_LORA + KV_LORA]),
            ka=_dot(h, wkr_ref[:, 0:LANES]),
            kb=_dot(h, wkr_ref[:, LANES:2 * LANES]),
            u=_dot(h, wuv_ref[:, 0:SGU_OUT]),
            v=_dot(h, wuv_ref[:, SGU_OUT:2 * SGU_OUT]))

    def mla_part(sb, raw):
        rows = slice(sb * FRONT_SUB, (sb + 1) * FRONT_SUB)
        cos, sin = _rope_tables(pos_ref, sb * FRONT_SUB, invf_ref[...], lane_group)
        sin = sin * sign_ref[...]

        cq = _rms(raw["cq"], qg_ref[...]).astype(BF16)
        q = _dot(cq, wuq_ref[...])
        cos_q = jnp.concatenate([cos] * n_rope_cols, axis=1)
        sin_q = jnp.concatenate([sin] * n_rope_cols, axis=1)
        qn = (q[:, :MLA_OUT] * scale).astype(BF16)
        qa = q[:, MLA_OUT:]
        qb = jnp.concatenate(
            [jnp.where(first_half, pltpu.roll(qa[:, c * LANES:(c + 1) * LANES], LANES - HALF_ROPE, 1),
                       pltpu.roll(qa[:, c * LANES:(c + 1) * LANES], HALF_ROPE, 1))
             for c in range(n_rope_cols)], axis=1)
        qr = ((qa * cos_q + qb * sin_q) * scale).astype(BF16)

        ckv = _rms(raw["ckv"], kvg_ref[...]).astype(BF16)
        kv = _dot(ckv, wukv_ref[...])
        kn = kv[:, :MLA_OUT].astype(BF16)
        v_ref[rows, :] = kv[:, MLA_OUT:].astype(BF16)

        kr = (raw["ka"] * cos + raw["kb"] * sin).astype(BF16)
        for hd in range(MLA_HEADS):
            lo = hd * HEAD_SLOT
            src = slice(hd * QK_NOPE, (hd + 1) * QK_NOPE)
            pair = hd % n_rope_cols
            rope = qr[:, pair * LANES:(pair + 1) * LANES]
            keep = low_lanes if hd < n_rope_cols else ~low_lanes
            q_ref[rows, lo:lo + QK_NOPE] = qn[:, src]
            q_ref[rows, lo + QK_NOPE:lo + HEAD_SLOT] = jnp.where(keep, rope, jnp.zeros_like(rope))
            k_ref[rows, lo:lo + QK_NOPE] = kn[:, src]
            k_ref[rows, lo + QK_NOPE:lo + HEAD_SLOT] = kr

    def sgu_norm(sb, raw):
        rows = slice(sb * FRONT_SUB, (sb + 1) * FRONT_SUB)
        u = jax.nn.gelu(raw["u"])
        vv = jax.nn.gelu(raw["v"])
        for g in range(SGU_GROUPS):
            lo = g * SGU_CH
            vg = vv[:, lo:lo + SGU_CH]
            mu = jnp.mean(vg, axis=-1, keepdims=True)
            d = vg - mu
            var = jnp.mean(d * d, axis=-1, keepdims=True)
            vn_buf[rows, lo:lo + SGU_CH] = (d * lax.rsqrt(var + EPS)
                                            * vng_ref[:, lo:lo + SGU_CH]).astype(BF16)
        return u

    def sgu_mix(sb, u):
        rows = slice(sb * FRONT_SUB, (sb + 1) * FRONT_SUB)
        for g in range(SGU_GROUPS):
            lo = g * SGU_CH
            for c in range(FRONT_SUB // CHUNK):
                t0 = sb * FRONT_SUB + c * CHUNK
                y = _dot(sw_buf[g], vn_buf[t0:t0 + CHUNK, lo:lo + SGU_CH])
                s_buf[t0:t0 + CHUNK, lo:lo + SGU_CH] = y + sgub_ref[:, lo:lo + SGU_CH]
        s_ref[rows, :] = _rms(u * s_buf[rows, :], sgn_ref[...]).astype(BF16)

    subs = list(range(x_ref.shape[0] // FRONT_SUB))
    raw = first_dots(normed(subs[0]))
    for sb in subs:
        nxt = first_dots(normed(sb + 1)) if sb + 1 < len(subs) else None
        mla_part(sb, raw)
        sgu_mix(sb, sgu_norm(sb, raw))
        raw = nxt
    _run_cast_riders(rider_in, rider_out)


def _even_front(x, pos, invf, sign, g, wlat, wuv, wkr, qg, wuq, kvg, wukv, vng, sguw, sgub, sgn,
                riders):
    T = x.shape[0]
    tm = FRONT_ROWS
    steps = T // tm
    row = lambda w: pl.BlockSpec((tm, w), lambda i: (i, 0))
    out_widths = (MLA_HEADS * HEAD_SLOT, MLA_HEADS * HEAD_SLOT, MLA_OUT, SGU_OUT)
    r_in, r_out, r_shapes = _cast_rider_specs(riders, steps, lambda i: i)
    consts = (invf, sign, g, wlat, wuv, wkr, qg, wuq, kvg, wukv, vng, sguw, sgub, sgn)
    return pl.pallas_call(
        functools.partial(_even_front_kernel, n_riders=len(riders)),
        grid=(steps,),
        in_specs=[row(D_MODEL), row(1)] + [_const_spec(c.shape) for c in consts] + r_in,
        out_specs=[row(w) for w in out_widths] + r_out,
        out_shape=[jax.ShapeDtypeStruct((T, w), BF16) for w in out_widths] + r_shapes,
        scratch_shapes=[pltpu.VMEM((tm, SGU_OUT), BF16), pltpu.VMEM((tm, SGU_OUT), F32),
                        pltpu.VMEM((SGU_GROUPS, CHUNK, CHUNK), BF16)],
        compiler_params=pltpu.CompilerParams(dimension_semantics=("arbitrary",),
                                             vmem_limit_bytes=VMEM_LIMIT_BYTES),
        name="even_front",
    )(x, pos, *consts, *[w for w, _ in riders])


def _rope_rows():
    lane = jnp.arange(LANES)
    inv_freq = ROPE_BASE ** (-jnp.arange(0, QK_ROPE, 2, dtype=F32) / QK_ROPE)
    invf = inv_freq[lane % HALF_ROPE][None, :]
    sign = jnp.where((lane % QK_ROPE) < HALF_ROPE, -1.0, 1.0).astype(F32)[None, :]
    return invf, sign


def _attn_kernel(*refs, n_riders):
    (q_ref, k_ref, v_ref), rider_in, (o_ref,), rider_out, (v_ext,) = _split_refs(refs, 3, 1, n_riders)
    seq = q_ref.shape[0]
    tq = ATTN_BLOCK
    row = lax.broadcasted_iota(jnp.int32, (tq, tq), 0)
    col = lax.broadcasted_iota(jnp.int32, (tq, tq), 1)
    causal = col <= row
    for hd in range(ATTN_HEADS):
        v_ext[hd, :, 0:V_HEAD] = v_ref[:, hd * V_HEAD:(hd + 1) * V_HEAD]
        v_ext[hd, :, V_HEAD:2 * V_HEAD] = jnp.ones((seq, V_HEAD), BF16)

    def scores(hd, qi):
        r0 = qi * tq
        cols = slice(hd * HEAD_SLOT, (hd + 1) * HEAD_SLOT)
        q = q_ref[r0:r0 + tq, cols]
        sd = jnp.where(causal, _dot_nt(q, k_ref[r0:r0 + tq, cols]), -jnp.inf)
        sp = _dot_nt(q, k_ref[0:r0, cols]) if qi else None
        return sd, sp

    order = [(hd, qi) for hd in range(ATTN_HEADS) for qi in reversed(range(seq // tq))]
    ahead = [scores(*item) for item in order[:ATTN_AHEAD]]
    for n, (hd, qi) in enumerate(order):
        r0 = qi * tq
        r1 = r0 + tq
        sd, sp = ahead.pop(0)
        if n + ATTN_AHEAD < len(order):
            ahead.append(scores(*order[n + ATTN_AHEAD]))
        m = jnp.max(sd, axis=-1, keepdims=True)
        if qi:
            m = jnp.maximum(m, jnp.max(sp, axis=-1, keepdims=True))
        acc = _dot(jnp.exp2(sd - m).astype(BF16), v_ext[hd, r0:r1, :])
        if qi:
            acc = acc + _dot(jnp.exp2(sp - m).astype(BF16), v_ext[hd, 0:r0, :])
        o_ref[r0:r1, hd * V_HEAD:(hd + 1) * V_HEAD] = (
            acc[:, 0:V_HEAD] / acc[:, V_HEAD:2 * V_HEAD]).astype(BF16)
    _run_cast_riders(rider_in, rider_out)


def _attention(q, k, v, batch, seq, riders):
    T = q.shape[0]
    groups = MLA_HEADS // ATTN_HEADS
    spec = lambda w: pl.BlockSpec((seq, ATTN_HEADS * w), lambda b, h: (b, h))
    r_in, r_out, r_shapes = _cast_rider_specs(riders, batch * groups, lambda b, h: b * groups + h)
    return pl.pallas_call(
        functools.partial(_attn_kernel, n_riders=len(riders)),
        grid=(batch, groups),
        in_specs=[spec(HEAD_SLOT), spec(HEAD_SLOT), spec(V_HEAD)] + r_in,
        out_specs=[spec(V_HEAD)] + r_out,
        out_shape=[jax.ShapeDtypeStruct((T, MLA_OUT), BF16)] + r_shapes,
        scratch_shapes=[pltpu.VMEM((ATTN_HEADS, seq, 2 * V_HEAD), BF16)],
        compiler_params=pltpu.CompilerParams(dimension_semantics=("parallel", "parallel"),
                                             vmem_limit_bytes=VMEM_LIMIT_BYTES),
        name="mla_attention",
    )(q, k, v, *[w for w, _ in riders])


def _out_mlp_kernel(x_ref, ma_ref, mb_ref, wo_ref, g_ref, w1_hbm, w2_hbm, fg_ref, *rest,
                    norm_a, final_norm):
    if norm_a:
        ag_ref, o_ref, h_buf, w1_buf, w2_buf, sem = rest
    else:
        o_ref, h_buf, w1_buf, w2_buf, sem = rest
    i = pl.program_id(0)
    half = ma_ref.shape[1]
    tf = w1_buf.shape[2]
    n_chunks = w1_hbm.shape[1] // tf
    assert n_chunks % MLP_SLOTS == 0, "chunk 0 must land in slot 0 of every row tile"

    def chunk_copies(k):
        slot = k % MLP_SLOTS
        return (pltpu.make_async_copy(w1_hbm.at[:, k * tf:(k + 1) * tf], w1_buf.at[slot],
                                      sem.at[0, slot]),
                pltpu.make_async_copy(w2_hbm.at[k * tf:(k + 1) * tf, :], w2_buf.at[slot],
                                      sem.at[1, slot]))

    @pl.when(i == 0)
    def _():
        for c in chunk_copies(0):
            c.start()

    for sb in range(x_ref.shape[0] // MLP_SUB):
        rows = slice(sb * MLP_SUB, (sb + 1) * MLP_SUB)
        ma = ma_ref[rows, :]
        if norm_a:
            ma = _rms(ma.astype(F32), ag_ref[...]).astype(BF16)
        x1 = (x_ref[rows, :] + _dot(ma, wo_ref[0:half, :])
              + _dot(mb_ref[rows, :], wo_ref[half:2 * half, :]))
        o_ref[rows, :] = x1
        h_buf[rows, :] = _rms(x1, g_ref[...]).astype(BF16)

    for k in range(n_chunks):
        slot = k % MLP_SLOTS
        for c in chunk_copies((k + 1) % n_chunks):
            c.start()
        for c in chunk_copies(k):
            c.wait()
        a = jnp.maximum(_dot(h_buf[...], w1_buf[slot]), 0.0)
        o_ref[...] += _dot((a * a).astype(BF16), w2_buf[slot])

    if final_norm:
        o_ref[...] = _rms(o_ref[...], fg_ref[...])

    @pl.when(i == pl.num_programs(0) - 1)
    def _():
        for c in chunk_copies(0):
            c.wait()


def _out_mlp(x, ma, mb, wo, g, w1, w2, fg, a_gain, *, final_norm):
    T = x.shape[0]
    tm, tf = MLP_ROWS, MLP_FF
    half = wo.shape[0] // 2
    norm_a = a_gain is not None
    in_specs = [pl.BlockSpec((tm, D_MODEL), lambda i: (i, 0)),
                pl.BlockSpec((tm, half), lambda i: (i, 0)),
                pl.BlockSpec((tm, half), lambda i: (i, mb.shape[1] // half - 1)),
                _const_spec(wo.shape), _const_spec(g.shape),
                pl.BlockSpec(memory_space=pl.ANY), pl.BlockSpec(memory_space=pl.ANY),
                _const_spec(fg.shape)]
    args = [x, ma, mb, wo, g, w1, w2, fg]
    if norm_a:
        in_specs.append(_const_spec(a_gain.shape))
        args.append(a_gain)
    return pl.pallas_call(
        functools.partial(_out_mlp_kernel, norm_a=norm_a, final_norm=final_norm),
        grid=(T // tm,),
        in_specs=in_specs,
        out_specs=pl.BlockSpec((tm, D_MODEL), lambda i: (i, 0)),
        out_shape=jax.ShapeDtypeStruct((T, D_MODEL), F32),
        scratch_shapes=[pltpu.VMEM((tm, D_MODEL), BF16),
                        pltpu.VMEM((MLP_SLOTS, D_MODEL, tf), BF16),
                        pltpu.VMEM((MLP_SLOTS, tf, D_MODEL), BF16),
                        pltpu.SemaphoreType.DMA((2, MLP_SLOTS))],
        compiler_params=pltpu.CompilerParams(dimension_semantics=("arbitrary",),
                                             vmem_limit_bytes=VMEM_LIMIT_BYTES),
        name="out_proj_mlp",
    )(*args)


def _conv_front_kernel(*refs, n_riders, tiles_per_seq):
    ins, rider_in, (o_ref,), rider_out, (hist,) = _split_refs(refs, 4, 1, n_riders)
    x_ref, g_ref, w_ref, cw_ref = ins
    tm = x_ref.shape[0]
    tn = CONV_COLS

    @pl.when(pl.program_id(0) == 0)
    def _():
        hist[...] = jnp.zeros_like(hist)

    seq_start = pl.program_id(0) % tiles_per_seq == 0
    h = _rms(x_ref[...], g_ref[...]).astype(BF16)
    row = lax.broadcasted_iota(jnp.int32, (tm, tn), 0)
    for cb in range(CONV_DIM // tn):
        c0 = cb * tn
        z = (_dot(h, w_ref[:, CONV_DIM + c0:CONV_DIM + c0 + tn])
             * _dot(h, w_ref[:, 2 * CONV_DIM + c0:2 * CONV_DIM + c0 + tn]))
        p1 = jnp.where(seq_start, 0.0, hist[F32_ROWS - 1:F32_ROWS, c0:c0 + tn])
        p2 = jnp.where(seq_start, 0.0, hist[F32_ROWS - 2:F32_ROWS - 1, c0:c0 + tn])
        z1 = jnp.where(row == 0, p1, pltpu.roll(z, 1, 0))
        z2 = jnp.where(row == 0, p2, jnp.where(row == 1, p1, pltpu.roll(z, 2, 0)))
        conv = (cw_ref[0:1, c0:c0 + tn] * z2 + cw_ref[1:2, c0:c0 + tn] * z1
                + cw_ref[2:3, c0:c0 + tn] * z)
        o_ref[:, c0:c0 + tn] = (_dot(h, w_ref[:, c0:c0 + tn]) * conv).astype(BF16)
        hist[:, c0:c0 + tn] = z[tm - F32_ROWS:tm, :]
    _run_cast_riders(rider_in, rider_out)


def _conv_front(x, g, w_in, conv_w, layer, seq, riders):
    T = x.shape[0]
    tm = CONV_ROWS
    steps = T // tm
    r_in, r_out, r_shapes = _cast_rider_specs(riders, steps, lambda i: i)
    return pl.pallas_call(
        functools.partial(_conv_front_kernel, n_riders=len(riders), tiles_per_seq=seq // tm),
        grid=(steps,),
        in_specs=[pl.BlockSpec((tm, D_MODEL), lambda i: (i, 0)), _const_spec(g.shape),
                  _const_spec(w_in.shape), _layer_spec(conv_w, layer)] + r_in,
        out_specs=[pl.BlockSpec((tm, CONV_DIM), lambda i: (i, 0))] + r_out,
        out_shape=[jax.ShapeDtypeStruct((T, CONV_DIM), BF16)] + r_shapes,
        scratch_shapes=[pltpu.VMEM((F32_ROWS, CONV_DIM), F32)],
        compiler_params=pltpu.CompilerParams(dimension_semantics=("arbitrary",),
                                             vmem_limit_bytes=VMEM_LIMIT_BYTES),
        name="conv_front",
    )(x, g, w_in, conv_w, *[w for w, _ in riders])


def _even_weights(w_in, w_uq, w_ukv):
    c1 = Q_LORA
    c2 = c1 + KV_LORA
    c3 = c2 + QK_ROPE
    k1 = w_in[:, c2:c2 + HALF_ROPE]
    k2 = w_in[:, c2 + HALF_ROPE:c3]
    wlat = w_in[:, :c2]
    wuv = w_in[:, c3:]
    wkr = jnp.concatenate([k1, k2, k1, k2, k2, k1, k2, k1], axis=1)
    wq = w_uq.reshape(Q_LORA, MLA_HEADS, QK_NOPE + QK_ROPE)
    r1 = wq[:, :, QK_NOPE:QK_NOPE + HALF_ROPE]
    r2 = wq[:, :, QK_NOPE + HALF_ROPE:]
    pairs = LANES // QK_ROPE

    def rope_cols(a, b):
        w = jnp.concatenate([a, b], axis=-1).reshape(Q_LORA, pairs, MLA_HEADS // pairs, QK_ROPE)
        return w.transpose(0, 2, 1, 3).reshape(Q_LORA, -1)

    wuq = jnp.concatenate([wq[:, :, :QK_NOPE].reshape(Q_LORA, -1), rope_cols(r1, r2)], axis=1)
    wkv = w_ukv.reshape(KV_LORA, MLA_HEADS, QK_NOPE + V_HEAD)
    wukv = jnp.concatenate([wkv[:, :, :QK_NOPE].reshape(KV_LORA, -1),
                            wkv[:, :, QK_NOPE:].reshape(KV_LORA, -1)], axis=1)
    return tuple(w.astype(BF16) for w in (wlat, wuv, wkr, wuq, wukv))


def kernel(x, positions, e_norm_mix, e_w_in, e_q_norm, e_w_uq, e_kv_norm, e_w_ukv, e_v_norm,
           e_sgu_w, e_sgu_b, e_mla_out_norm, e_sgu_out_norm, e_w_out, o_norm_mix, o_w_in,
           o_conv_w, o_w_out, mlp_norm, mlp_w1, mlp_w2, final_norm):
    batch, seq, d = x.shape
    T = batch * seq
    depth = mlp_norm.shape[0]
    assert d == D_MODEL and seq % ATTN_BLOCK == 0 and seq % CONV_ROWS == 0, x.shape
    assert T % FRONT_ROWS == 0 and T % MLP_ROWS == 0, x.shape
    xf = x.reshape(T, d)
    pos = positions.reshape(T, 1)
    invf, sign = _rope_rows()
    fg = final_norm[None, :]
    row2d = lambda a: a.reshape(1, -1)

    o_wi = None
    for layer in range(depth):
        i = layer // 2
        mlp_riders = [(mlp_w1, layer), (mlp_w2, layer)]
        if layer % 2 == 0:
            wlat, wuv, wkr, wuq, wukv = _even_weights(e_w_in[i], e_w_uq[i], e_w_ukv[i])
            sgub = jnp.repeat(e_sgu_b[i].T, SGU_CH, axis=1)
            q, k, v, sn, wo = _even_front(
                xf, pos, invf, sign, row2d(e_norm_mix[i]), wlat, wuv, wkr, row2d(e_q_norm[i]), wuq,
                row2d(e_kv_norm[i]), wukv, row2d(e_v_norm[i]), e_sgu_w[i], sgub,
                row2d(e_sgu_out_norm[i]), riders=[(e_w_out, i)])
            next_in = [(o_w_in, (layer + 1) // 2)] if layer + 1 < depth else []
            ma, w1, w2, *rest = _attention(q, k, v, batch, seq, mlp_riders + next_in)
            o_wi = rest[0] if rest else None
            mb, a_gain = sn, row2d(e_mla_out_norm[i])
        else:
            ma, wo, w1, w2 = _conv_front(xf, row2d(o_norm_mix[i]), o_wi, o_conv_w, i, seq,
                                         riders=[(o_w_out, i)] + mlp_riders)
            mb, a_gain = ma, None
        xf = _out_mlp(xf, ma, mb, wo, row2d(mlp_norm[layer]), w1, w2, fg, a_gain,
                      final_norm=(layer == depth - 1))
    return xf.reshape(batch, seq, d)
```

```python
import functools
import math

import jax
import jax.numpy as jnp
from jax import lax
from jax.experimental import pallas as pl
from jax.experimental.pallas import tpu as pltpu

D_MODEL = 2048
MLA_HEADS = 8
Q_LORA = 512
KV_LORA = 512
QK_NOPE = 128
QK_ROPE = 64
V_HEAD = 128
ROPE_BASE = 10000.0
SGU_GROUPS = 8
SGU_CH = 128
CHUNK = 128
CONV_DIM = D_MODEL
CONV_WIDTH = 3
D_FF = 4 * D_MODEL
EPS = 1e-6
MLA_OUT = MLA_HEADS * V_HEAD
SGU_OUT = SGU_GROUPS * SGU_CH
HALF_ROPE = QK_ROPE // 2
LOG2_E = math.log2(math.e)
HEAD_SLOT = 2 * QK_NOPE

LANES = 128
F32_ROWS = 8
BF16_ROWS = 16
VMEM_LIMIT_BYTES = 60 * 1024 * 1024

FRONT_ROWS = 512
FRONT_SUB = 256
ATTN_BLOCK = 256
ATTN_HEADS = 2
ATTN_AHEAD = 2
MLP_ROWS = 512
MLP_SUB = 256
MLP_FF = 1024
MLP_SLOTS = 2
CONV_ROWS = 512
CONV_COLS = 1024

BF16 = jnp.bfloat16
F32 = jnp.float32


def _dot(a, b):
    return jnp.dot(a, b, preferred_element_type=F32)


def _dot_nt(a, b):
    return lax.dot_general(a, b, (((1,), (1,)), ((), ())), preferred_element_type=F32)


def _rms(x, g):
    return x * lax.rsqrt(jnp.mean(x * x, axis=-1, keepdims=True) + EPS) * g


def _const_spec(shape):
    return pl.BlockSpec(shape, lambda *_: (0,) * len(shape), pipeline_mode=pl.Buffered(1))


def _layer_spec(w, layer):
    return pl.BlockSpec((None,) + w.shape[1:], lambda *_: (layer, 0, 0), pipeline_mode=pl.Buffered(1))


def _cast_rider_specs(riders, steps, step_of):
    ins, outs, shapes = [], [], []
    for w, layer in riders:
        _, rows, cols = w.shape
        rb = rows // steps
        assert rb * steps == rows and rb % BF16_ROWS == 0, (w.shape, steps)
        ins.append(pl.BlockSpec((None, rb, cols), lambda *g, layer=layer: (layer, step_of(*g), 0)))
        outs.append(pl.BlockSpec((rb, cols), lambda *g: (step_of(*g), 0)))
        shapes.append(jax.ShapeDtypeStruct((rows, cols), BF16))
    return ins, outs, shapes


def _split_refs(refs, n_in, n_out, n_riders):
    cuts = [n_in, n_in + n_riders, n_in + n_riders + n_out, n_in + 2 * n_riders + n_out]
    return (refs[:cuts[0]], refs[cuts[0]:cuts[1]], refs[cuts[1]:cuts[2]], refs[cuts[2]:cuts[3]],
            refs[cuts[3]:])


def _run_cast_riders(srcs, dsts):
    for src, dst in zip(srcs, dsts):
        dst[...] = src[...].astype(BF16)


def _rope_tables(pos_ref, r0, invf, lane_group):
    n = LANES // HALF_ROPE
    seg = lane_group.shape[0]
    packed = jnp.zeros((seg, LANES), F32)
    for j in range(n):
        p = pos_ref[r0 + j * seg:r0 + (j + 1) * seg, :].astype(F32)
        packed = jnp.where(lane_group == j, p, packed)
    ang = packed * invf
    tables = []
    for t in (jnp.cos(ang), jnp.sin(ang)):
        parts = []
        for j in range(n):
            one = jnp.where(lane_group == j, t, 0.0)
            full = one
            for s in range(1, n):
                full = full + pltpu.roll(one, s * HALF_ROPE, 1)
            parts.append(full)
        tables.append(jnp.concatenate(parts, axis=0))
    return tables


def _even_front_kernel(*refs, n_riders):
    ins, rider_in, outs, rider_out, (vn_buf, s_buf, sw_buf) = _split_refs(refs, 16, 4, n_riders)
    (x_ref, pos_ref, invf_ref, sign_ref, g_ref, wlat_ref, wuv_ref, wkr_ref, qg_ref, wuq_ref,
     kvg_ref, wukv_ref, vng_ref, sguw_ref, sgub_ref, sgn_ref) = ins
    q_ref, k_ref, v_ref, s_ref = outs

    @pl.when(pl.program_id(0) == 0)
    def _():
        row = lax.broadcasted_iota(jnp.int32, (CHUNK, CHUNK), 0)
        col = lax.broadcasted_iota(jnp.int32, (CHUNK, CHUNK), 1)
        for g in range(SGU_GROUPS):
            sw_buf[g] = jnp.where(col <= row, sguw_ref[g], 0.0).astype(BF16)

    scale = (QK_NOPE + QK_ROPE) ** -0.5 * LOG2_E
    n_rope_cols = MLA_HEADS * QK_ROPE // LANES
    lane = lax.broadcasted_iota(jnp.int32, (FRONT_SUB, LANES), 1)
    low_lanes = lane < QK_ROPE
    first_half = lane % QK_ROPE < HALF_ROPE
    seg = FRONT_SUB // (LANES // HALF_ROPE)
    lane_group = lax.broadcasted_iota(jnp.int32, (seg, LANES), 1) // HALF_ROPE

    def normed(sb):
        return _rms(x_ref[sb * FRONT_SUB:(sb + 1) * FRONT_SUB, :], g_ref[...]).astype(BF16)

    def first_dots(h):
        return dict(
            cq=_dot(h, wlat_ref[:, 0:Q_LORA]),
            ckv=_dot(h, wlat_ref[:, Q_LORA:Q_LORA + KV_LORA]),
            ka=_dot(h, wkr_ref[:, 0:LANES]),
            kb=_dot(h, wkr_ref[:, LANES:2 * LANES]),
            u=_dot(h, wuv_ref[:, 0:SGU_OUT]),
            v=_dot(h, wuv_ref[:, SGU_OUT:2 * SGU_OUT]))

    def mla_part(sb, raw):
        rows = slice(sb * FRONT_SUB, (sb + 1) * FRONT_SUB)
        cos, sin = _rope_tables(pos_ref, sb * FRONT_SUB, invf_ref[...], lane_group)
        sin = sin * sign_ref[...]

        cq = _rms(raw["cq"], qg_ref[...]).astype(BF16)
        q = _dot(cq, wuq_ref[...])
        cos_q = jnp.concatenate([cos] * n_rope_cols, axis=1)
        sin_q = jnp.concatenate([sin] * n_rope_cols, axis=1)
        qn = (q[:, :MLA_OUT] * scale).astype(BF16)
        qa = q[:, MLA_OUT:]
        qb = jnp.concatenate(
            [jnp.where(first_half, pltpu.roll(qa[:, c * LANES:(c + 1) * LANES], LANES - HALF_ROPE, 1),
                       pltpu.roll(qa[:, c * LANES:(c + 1) * LANES], HALF_ROPE, 1))
             for c in range(n_rope_cols)], axis=1)
        qr = ((qa * cos_q + qb * sin_q) * scale).astype(BF16)

        ckv = _rms(raw["ckv"], kvg_ref[...]).astype(BF16)
        kv = _dot(ckv, wukv_ref[...])
        kn = kv[:, :MLA_OUT].astype(BF16)
        v_ref[rows, :] = kv[:, MLA_OUT:].astype(BF16)

        kr = (raw["ka"] * cos + raw["kb"] * sin).astype(BF16)
        for hd in range(MLA_HEADS):
            lo = hd * HEAD_SLOT
            src = slice(hd * QK_NOPE, (hd + 1) * QK_NOPE)
            pair = hd % n_rope_cols
            rope = qr[:, pair * LANES:(pair + 1) * LANES]
            keep = low_lanes if hd < n_rope_cols else ~low_lanes
            q_ref[rows, lo:lo + QK_NOPE] = qn[:, src]
            q_ref[rows, lo + QK_NOPE:lo + HEAD_SLOT] = jnp.where(keep, rope, jnp.zeros_like(rope))
            k_ref[rows, lo:lo + QK_NOPE] = kn[:, src]
            k_ref[rows, lo + QK_NOPE:lo + HEAD_SLOT] = kr

    def sgu_norm(sb, raw):
        rows = slice(sb * FRONT_SUB, (sb + 1) * FRONT_SUB)
        u = jax.nn.gelu(raw["u"])
        vv = jax.nn.gelu(raw["v"])
        for g in range(SGU_GROUPS):
            lo = g * SGU_CH
            vg = vv[:, lo:lo + SGU_CH]
            mu = jnp.mean(vg, axis=-1, keepdims=True)
            d = vg - mu
            var = jnp.mean(d * d, axis=-1, keepdims=True)
            vn_buf[rows, lo:lo + SGU_CH] = (d * lax.rsqrt(var + EPS)
                                            * vng_ref[:, lo:lo + SGU_CH]).astype(BF16)
        return u

    def sgu_mix(sb, u):
        rows = slice(sb * FRONT_SUB, (sb + 1) * FRONT_SUB)
        for g in range(SGU_GROUPS):
            lo = g * SGU_CH
            for c in range(FRONT_SUB // CHUNK):
                t0 = sb * FRONT_SUB + c * CHUNK
                y = _dot(sw_buf[g], vn_buf[t0:t0 + CHUNK, lo:lo + SGU_CH])
                s_buf[t0:t0 + CHUNK, lo:lo + SGU_CH] = y + sgub_ref[:, lo:lo + SGU_CH]
        s_ref[rows, :] = _rms(u * s_buf[rows, :], sgn_ref[...]).astype(BF16)

    subs = list(range(x_ref.shape[0] // FRONT_SUB))
    raw = first_dots(normed(subs[0]))
    for sb in subs:
        nxt = first_dots(normed(sb + 1)) if sb + 1 < len(subs) else None
        mla_part(sb, raw)
        sgu_mix(sb, sgu_norm(sb, raw))
        raw = nxt
    _run_cast_riders(rider_in, rider_out)


def _even_front(x, pos, invf, sign, g, wlat, wuv, wkr, qg, wuq, kvg, wukv, vng, sguw, sgub, sgn,
                riders):
    T = x.shape[0]
    tm = FRONT_ROWS
    steps = T // tm
    row = lambda w: pl.BlockSpec((tm, w), lambda i: (i, 0))
    out_widths = (MLA_HEADS * HEAD_SLOT, MLA_HEADS * HEAD_SLOT, MLA_OUT, SGU_OUT)
    r_in, r_out, r_shapes = _cast_rider_specs(riders, steps, lambda i: i)
    consts = (invf, sign, g, wlat, wuv, wkr, qg, wuq, kvg, wukv, vng, sguw, sgub, sgn)
    return pl.pallas_call(
        functools.partial(_even_front_kernel, n_riders=len(riders)),
        grid=(steps,),
        in_specs=[row(D_MODEL), row(1)] + [_const_spec(c.shape) for c in consts] + r_in,
        out_specs=[row(w) for w in out_widths] + r_out,
        out_shape=[jax.ShapeDtypeStruct((T, w), BF16) for w in out_widths] + r_shapes,
        scratch_shapes=[pltpu.VMEM((tm, SGU_OUT), BF16), pltpu.VMEM((tm, SGU_OUT), F32),
                        pltpu.VMEM((SGU_GROUPS, CHUNK, CHUNK), BF16)],
        compiler_params=pltpu.CompilerParams(dimension_semantics=("arbitrary",),
                                             vmem_limit_bytes=VMEM_LIMIT_BYTES),
        name="even_front",
    )(x, pos, *consts, *[w for w, _ in riders])


def _rope_rows():
    lane = jnp.arange(LANES)
    inv_freq = ROPE_BASE ** (-jnp.arange(0, QK_ROPE, 2, dtype=F32) / QK_ROPE)
    invf = inv_freq[lane % HALF_ROPE][None, :]
    sign = jnp.where((lane % QK_ROPE) < HALF_ROPE, -1.0, 1.0).astype(F32)[None, :]
    return invf, sign


def _attn_kernel(*refs, n_riders):
    (q_ref, k_ref, v_ref), rider_in, (o_ref,), rider_out, (vt_buf,) = _split_refs(refs, 3, 1, n_riders)
    seq = q_ref.shape[0]
    tq = ATTN_BLOCK
    key = lax.broadcasted_iota(jnp.int32, (tq, tq), 0)
    qry = lax.broadcasted_iota(jnp.int32, (tq, tq), 1)
    causal = key <= qry
    for hd in range(ATTN_HEADS):
        vt_buf[hd] = v_ref[:, hd * V_HEAD:(hd + 1) * V_HEAD].T

    def scores(hd, qi):
        r0 = qi * tq
        cols = slice(hd * HEAD_SLOT, (hd + 1) * HEAD_SLOT)
        q = q_ref[r0:r0 + tq, cols]
        sd = jnp.where(causal, _dot_nt(k_ref[r0:r0 + tq, cols], q), -jnp.inf)
        sp = _dot_nt(k_ref[0:r0, cols], q) if qi else None
        return sd, sp

    order = [(hd, qi) for hd in range(ATTN_HEADS) for qi in reversed(range(seq // tq))]
    ahead = [scores(*item) for item in order[:ATTN_AHEAD]]
    for n, (hd, qi) in enumerate(order):
        r0 = qi * tq
        r1 = r0 + tq
        sd, sp = ahead.pop(0)
        if n + ATTN_AHEAD < len(order):
            ahead.append(scores(*order[n + ATTN_AHEAD]))
        m = jnp.max(sd, axis=0, keepdims=True)
        if qi:
            m = jnp.maximum(m, jnp.max(sp, axis=0, keepdims=True))
        pd = jnp.exp2(sd - m)
        l = jnp.sum(pd, axis=0, keepdims=True)
        acc = _dot(vt_buf[hd, :, r0:r1], pd.astype(BF16))
        if qi:
            pp = jnp.exp2(sp - m)
            l = l + jnp.sum(pp, axis=0, keepdims=True)
            acc = acc + _dot(vt_buf[hd, :, 0:r0], pp.astype(BF16))
        o_ref[r0:r1, hd * V_HEAD:(hd + 1) * V_HEAD] = (acc / l).T.astype(BF16)
    _run_cast_riders(rider_in, rider_out)


def _attention(q, k, v, batch, seq, riders):
    T = q.shape[0]
    groups = MLA_HEADS // ATTN_HEADS
    spec = lambda w: pl.BlockSpec((seq, ATTN_HEADS * w), lambda b, h: (b, h))
    r_in, r_out, r_shapes = _cast_rider_specs(riders, batch * groups, lambda b, h: b * groups + h)
    return pl.pallas_call(
        functools.partial(_attn_kernel, n_riders=len(riders)),
        grid=(batch, groups),
        in_specs=[spec(HEAD_SLOT), spec(HEAD_SLOT), spec(V_HEAD)] + r_in,
        out_specs=[spec(V_HEAD)] + r_out,
        out_shape=[jax.ShapeDtypeStruct((T, MLA_OUT), BF16)] + r_shapes,
        scratch_shapes=[pltpu.VMEM((ATTN_HEADS, V_HEAD, seq), BF16)],
        compiler_params=pltpu.CompilerParams(dimension_semantics=("parallel", "parallel"),
                                             vmem_limit_bytes=VMEM_LIMIT_BYTES),
        name="mla_attention",
    )(q, k, v, *[w for w, _ in riders])


def _out_mlp_kernel(x_ref, ma_ref, mb_ref, wo_ref, g_ref, w1_hbm, w2_hbm, fg_ref, *rest,
                    norm_a, final_norm):
    if norm_a:
        ag_ref, o_ref, h_buf, w1_buf, w2_buf, sem = rest
    else:
        o_ref, h_buf, w1_buf, w2_buf, sem = rest
    i = pl.program_id(0)
    half = ma_ref.shape[1]
    tf = w1_buf.shape[2]
    n_chunks = w1_hbm.shape[1] // tf
    assert n_chunks % MLP_SLOTS == 0, "chunk 0 must land in slot 0 of every row tile"

    def chunk_copies(k):
        slot = k % MLP_SLOTS
        return (pltpu.make_async_copy(w1_hbm.at[:, k * tf:(k + 1) * tf], w1_buf.at[slot],
                                      sem.at[0, slot]),
                pltpu.make_async_copy(w2_hbm.at[k * tf:(k + 1) * tf, :], w2_buf.at[slot],
                                      sem.at[1, slot]))

    @pl.when(i == 0)
    def _():
        for c in chunk_copies(0):
            c.start()

    for sb in range(x_ref.shape[0] // MLP_SUB):
        rows = slice(sb * MLP_SUB, (sb + 1) * MLP_SUB)
        ma = ma_ref[rows, :]
        if norm_a:
            ma = _rms(ma.astype(F32), ag_ref[...]).astype(BF16)
        x1 = (x_ref[rows, :] + _dot(ma, wo_ref[0:half, :])
              + _dot(mb_ref[rows, :], wo_ref[half:2 * half, :]))
        o_ref[rows, :] = x1
        h_buf[rows, :] = _rms(x1, g_ref[...]).astype(BF16)

    for k in range(n_chunks):
        slot = k % MLP_SLOTS
        for c in chunk_copies((k + 1) % n_chunks):
            c.start()
        for c in chunk_copies(k):
            c.wait()
        a = jnp.maximum(_dot(h_buf[...], w1_buf[slot]), 0.0)
        o_ref[...] += _dot((a * a).astype(BF16), w2_buf[slot])

    if final_norm:
        o_ref[...] = _rms(o_ref[...], fg_ref[...])

    @pl.when(i == pl.num_programs(0) - 1)
    def _():
        for c in chunk_copies(0):
            c.wait()


def _out_mlp(x, ma, mb, wo, g, w1, w2, fg, a_gain, *, final_norm):
    T = x.shape[0]
    tm, tf = MLP_ROWS, MLP_FF
    half = wo.shape[0] // 2
    norm_a = a_gain is not None
    in_specs = [pl.BlockSpec((tm, D_MODEL), lambda i: (i, 0)),
                pl.BlockSpec((tm, half), lambda i: (i, 0)),
                pl.BlockSpec((tm, half), lambda i: (i, mb.shape[1] // half - 1)),
                _const_spec(wo.shape), _const_spec(g.shape),
                pl.BlockSpec(memory_space=pl.ANY), pl.BlockSpec(memory_space=pl.ANY),
                _const_spec(fg.shape)]
    args = [x, ma, mb, wo, g, w1, w2, fg]
    if norm_a:
        in_specs.append(_const_spec(a_gain.shape))
        args.append(a_gain)
    return pl.pallas_call(
        functools.partial(_out_mlp_kernel, norm_a=norm_a, final_norm=final_norm),
        grid=(T // tm,),
        in_specs=in_specs,
        out_specs=pl.BlockSpec((tm, D_MODEL), lambda i: (i, 0)),
        out_shape=jax.ShapeDtypeStruct((T, D_MODEL), F32),
        scratch_shapes=[pltpu.VMEM((tm, D_MODEL), BF16),
                        pltpu.VMEM((MLP_SLOTS, D_MODEL, tf), BF16),
                        pltpu.VMEM((MLP_SLOTS, tf, D_MODEL), BF16),
                        pltpu.SemaphoreType.DMA((2, MLP_SLOTS))],
        compiler_params=pltpu.CompilerParams(dimension_semantics=("arbitrary",),
                                             vmem_limit_bytes=VMEM_LIMIT_BYTES),
        name="out_proj_mlp",
    )(*args)


def _conv_front_kernel(*refs, n_riders, tiles_per_seq):
    ins, rider_in, (o_ref,), rider_out, (hist,) = _split_refs(refs, 4, 1, n_riders)
    x_ref, g_ref, w_ref, cw_ref = ins
    tm = x_ref.shape[0]
    tn = CONV_COLS

    @pl.when(pl.program_id(0) == 0)
    def _():
        hist[...] = jnp.zeros_like(hist)

    seq_start = pl.program_id(0) % tiles_per_seq == 0
    h = _rms(x_ref[...], g_ref[...]).astype(BF16)
    row = lax.broadcasted_iota(jnp.int32, (tm, tn), 0)
    for cb in range(CONV_DIM // tn):
        c0 = cb * tn
        z = (_dot(h, w_ref[:, CONV_DIM + c0:CONV_DIM + c0 + tn])
             * _dot(h, w_ref[:, 2 * CONV_DIM + c0:2 * CONV_DIM + c0 + tn]))
        p1 = jnp.where(seq_start, 0.0, hist[F32_ROWS - 1:F32_ROWS, c0:c0 + tn])
        p2 = jnp.where(seq_start, 0.0, hist[F32_ROWS - 2:F32_ROWS - 1, c0:c0 + tn])
        z1 = jnp.where(row == 0, p1, pltpu.roll(z, 1, 0))
        z2 = jnp.where(row == 0, p2, jnp.where(row == 1, p1, pltpu.roll(z, 2, 0)))
        conv = (cw_ref[0:1, c0:c0 + tn] * z2 + cw_ref[1:2, c0:c0 + tn] * z1
                + cw_ref[2:3, c0:c0 + tn] * z)
        o_ref[:, c0:c0 + tn] = (_dot(h, w_ref[:, c0:c0 + tn]) * conv).astype(BF16)
        hist[:, c0:c0 + tn] = z[tm - F32_ROWS:tm, :]
    _run_cast_riders(rider_in, rider_out)


def _conv_front(x, g, w_in, conv_w, layer, seq, riders):
    T = x.shape[0]
    tm = CONV_ROWS
    steps = T // tm
    r_in, r_out, r_shapes = _cast_rider_specs(riders, steps, lambda i: i)
    return pl.pallas_call(
        functools.partial(_conv_front_kernel, n_riders=len(riders), tiles_per_seq=seq // tm),
        grid=(steps,),
        in_specs=[pl.BlockSpec((tm, D_MODEL), lambda i: (i, 0)), _const_spec(g.shape),
                  _const_spec(w_in.shape), _layer_spec(conv_w, layer)] + r_in,
        out_specs=[pl.BlockSpec((tm, CONV_DIM), lambda i: (i, 0))] + r_out,
        out_shape=[jax.ShapeDtypeStruct((T, CONV_DIM), BF16)] + r_shapes,
        scratch_shapes=[pltpu.VMEM((F32_ROWS, CONV_DIM), F32)],
        compiler_params=pltpu.CompilerParams(dimension_semantics=("arbitrary",),
                                             vmem_limit_bytes=VMEM_LIMIT_BYTES),
        name="conv_front",
    )(x, g, w_in, conv_w, *[w for w, _ in riders])


def _even_weights(w_in, w_uq, w_ukv):
    c1 = Q_LORA
    c2 = c1 + KV_LORA
    c3 = c2 + QK_ROPE
    k1 = w_in[:, c2:c2 + HALF_ROPE]
    k2 = w_in[:, c2 + HALF_ROPE:c3]
    wlat = w_in[:, :c2]
    wuv = w_in[:, c3:]
    wkr = jnp.concatenate([k1, k2, k1, k2, k2, k1, k2, k1], axis=1)
    wq = w_uq.reshape(Q_LORA, MLA_HEADS, QK_NOPE + QK_ROPE)
    r1 = wq[:, :, QK_NOPE:QK_NOPE + HALF_ROPE]
    r2 = wq[:, :, QK_NOPE + HALF_ROPE:]
    pairs = LANES // QK_ROPE

    def rope_cols(a, b):
        w = jnp.concatenate([a, b], axis=-1).reshape(Q_LORA, pairs, MLA_HEADS // pairs, QK_ROPE)
        return w.transpose(0, 2, 1, 3).reshape(Q_LORA, -1)

    wuq = jnp.concatenate([wq[:, :, :QK_NOPE].reshape(Q_LORA, -1), rope_cols(r1, r2)], axis=1)
    wkv = w_ukv.reshape(KV_LORA, MLA_HEADS, QK_NOPE + V_HEAD)
    wukv = jnp.concatenate([wkv[:, :, :QK_NOPE].reshape(KV_LORA, -1),
                            wkv[:, :, QK_NOPE:].reshape(KV_LORA, -1)], axis=1)
    return tuple(w.astype(BF16) for w in (wlat, wuv, wkr, wuq, wukv))


def kernel(x, positions, e_norm_mix, e_w_in, e_q_norm, e_w_uq, e_kv_norm, e_w_ukv, e_v_norm,
           e_sgu_w, e_sgu_b, e_mla_out_norm, e_sgu_out_norm, e_w_out, o_norm_mix, o_w_in,
           o_conv_w, o_w_out, mlp_norm, mlp_w1, mlp_w2, final_norm):
    batch, seq, d = x.shape
    T = batch * seq
    depth = mlp_norm.shape[0]
    assert d == D_MODEL and seq % ATTN_BLOCK == 0 and seq % CONV_ROWS == 0, x.shape
    assert T % FRONT_ROWS == 0 and T % MLP_ROWS == 0, x.shape
    xf = x.reshape(T, d)
    pos = positions.reshape(T, 1)
    invf, sign = _rope_rows()
    fg = final_norm[None, :]
    row2d = lambda a: a.reshape(1, -1)

    o_wi = None
    for layer in range(depth):
        i = layer // 2
        mlp_riders = [(mlp_w1, layer), (mlp_w2, layer)]
        if layer % 2 == 0:
            wlat, wuv, wkr, wuq, wukv = _even_weights(e_w_in[i], e_w_uq[i], e_w_ukv[i])
            sgub = jnp.repeat(e_sgu_b[i].T, SGU_CH, axis=1)
            q, k, v, sn, wo = _even_front(
                xf, pos, invf, sign, row2d(e_norm_mix[i]), wlat, wuv, wkr, row2d(e_q_norm[i]), wuq,
                row2d(e_kv_norm[i]), wukv, row2d(e_v_norm[i]), e_sgu_w[i], sgub,
                row2d(e_sgu_out_norm[i]), riders=[(e_w_out, i)])
            next_in = [(o_w_in, (layer + 1) // 2)] if layer + 1 < depth else []
            ma, w1, w2, *rest = _attention(q, k, v, batch, seq, mlp_riders + next_in)
            o_wi = rest[0] if rest else None
            mb, a_gain = sn, row2d(e_mla_out_norm[i])
        else:
            ma, wo, w1, w2 = _conv_front(xf, row2d(o_norm_mix[i]), o_wi, o_conv_w, i, seq,
                                         riders=[(o_w_out, i)] + mlp_riders)
            mb, a_gain = ma, None
        xf = _out_mlp(xf, ma, mb, wo, row2d(mlp_norm[layer]), w1, w2, fg, a_gain,
                      final_norm=(layer == depth - 1))
    return xf.reshape(batch, seq, d)
```
